```python
import jax
import jax.numpy as jnp
from jax import lax
import numpy as np

D_MODEL = 1024
BATCH = 2
SEQ = 8192
DEPTH = 2

HQ_A = 8
HKV_A = 2
DH_A = 64
WINDOW = 128
BLOCK_A = 128
H_B = 4
DK_B = 128
DV_B = 256
GK_RANK = 16
GATE_NORMALIZER = 16.0
CHUNK_B = 64
SUB_B = 16
D_FF = 2816
CONV_W = 3
EPS = 1e-6

QA_W = HQ_A * DH_A
KVA_W = HKV_A * DH_A
QKB_W = H_B * DK_B
VB_W = H_B * DV_B
SPLITS = (QA_W, KVA_W, KVA_W, QKB_W, QKB_W, VB_W, VB_W, GK_RANK, D_MODEL, D_MODEL)
IN_W = QA_W + 2 * KVA_W + 2 * QKB_W + 2 * VB_W + GK_RANK + 2 * D_MODEL

kernel_name = "hybrid_swa_sink_gla_convffn"


def rmsnorm(x, g):
    xf = x.astype(jnp.float32)
    y = xf * lax.rsqrt(jnp.mean(xf * xf, axis=-1, keepdims=True) + EPS)
    return (y * g.astype(jnp.float32)).astype(x.dtype)


def split_cols(z):
    idx = []
    acc = 0
    for w in SPLITS[:-1]:
        acc += w
        idx.append(acc)
    return jnp.split(z, idx, axis=-1)


def sliding_window_sink_attention(q, k, v, sinks):
    B, S = q.shape[0], q.shape[1]
    nb = S // BLOCK_A
    G = HQ_A // HKV_A
    qb = q.reshape(B, nb, BLOCK_A, HKV_A, G, DH_A)
    kb = k.reshape(B, nb, BLOCK_A, HKV_A, DH_A)
    vb = v.reshape(B, nb, BLOCK_A, HKV_A, DH_A)
    shift = lambda t: jnp.concatenate([jnp.zeros_like(t[:, :1]), t[:, :-1]], axis=1)
    kk = jnp.concatenate([shift(kb), kb], axis=2)
    vv = jnp.concatenate([shift(vb), vb], axis=2)
    s = jnp.einsum("bnqhgd,bnkhd->bnhgqk", qb, kk).astype(jnp.float32) * (DH_A ** -0.5)
    qpos = jnp.arange(BLOCK_A)[:, None]
    kpos = jnp.arange(2 * BLOCK_A)[None, :] - BLOCK_A
    band = (kpos <= qpos) & (kpos > qpos - WINDOW)
    kabs = (jnp.arange(nb) * BLOCK_A)[:, None, None] + kpos[None]
    mask = band[None] & (kabs >= 0)
    s = jnp.where(mask[None, :, None, None], s, -jnp.inf)
    sink = sinks.astype(jnp.float32).reshape(HKV_A, G)[None, None, :, :, None, None]
    m = jnp.maximum(jnp.max(s, axis=-1, keepdims=True), sink)
    p = jnp.exp(s - m)
    p = p / (jnp.sum(p, axis=-1, keepdims=True) + jnp.exp(sink - m))
    o = jnp.einsum("bnhgqk,bnkhd->bnqhgd", p.astype(v.dtype), vv)
    return o.reshape(B, S, QA_W)


def gla_chunked(q, k, v, gk):
    out_dtype = v.dtype
    f32 = jnp.float32
    B, S = q.shape[0], q.shape[1]
    N = S // CHUNK_B
    NS = CHUNK_B // SUB_B
    to_chunks = lambda t: t.astype(f32).reshape(B, N, CHUNK_B, H_B, t.shape[-1]).transpose(0, 3, 1, 2, 4)
    q = to_chunks(q) * (DK_B ** -0.5)
    k = to_chunks(k)
    v = to_chunks(v)
    b = jnp.cumsum(to_chunks(gk), axis=3)
    b_last = b[:, :, :, -1:]
    kv = jnp.einsum("bhncd,bhnce->bhnde", k * jnp.exp(b_last - b), v)
    decay = jnp.exp(b_last[:, :, :, 0])

    def step(state, inp):
        kv_n, d_n = inp
        return d_n[..., None] * state + kv_n, state

    _, s_prev = lax.scan(step, jnp.zeros((B, H_B, DK_B, DV_B), f32),
                         (jnp.moveaxis(kv, 2, 0), jnp.moveaxis(decay, 2, 0)))
    s_prev = jnp.moveaxis(s_prev, 0, 2)
    o_inter = jnp.einsum("bhncd,bhnde->bhnce", q * jnp.exp(b), s_prev)
    sub = lambda t: t.reshape(B, H_B, N, NS, SUB_B, t.shape[-1])
    qs, ks, bs = sub(q), sub(k), sub(b)
    r = bs[:, :, :, :, -1]
    lower = jnp.arange(NS)[:, None] > jnp.arange(NS)[None, :]
    dq = jnp.where(lower[:, :, None, None], bs[:, :, :, :, None] - r[:, :, :, None, :, None], 0.0)
    q_ref = qs[:, :, :, :, None] * jnp.exp(dq)
    k_ref = ks * jnp.exp(r[:, :, :, :, None] - bs)
    a_off = jnp.einsum("bhnijtd,bhnjsd->bhnijts", q_ref, k_ref)
    a_off = jnp.where(lower[:, :, None, None], a_off, 0.0)
    tri = jnp.arange(SUB_B)[:, None] >= jnp.arange(SUB_B)[None, :]
    dd = jnp.where(tri[:, :, None], bs[..., :, None, :] - bs[..., None, :, :], 0.0)
    a_diag = jnp.einsum("bhnitsd,bhnisd->bhnits", qs[..., :, None, :] * jnp.exp(dd), ks)
    a_diag = jnp.where(tri, a_diag, 0.0)
    a_blocks = a_off + jnp.eye(NS, dtype=f32)[:, :, None, None] * a_diag[:, :, :, :, None]
    a_intra = a_blocks.transpose(0, 1, 2, 3, 5, 4, 6).reshape(B, H_B, N, CHUNK_B, CHUNK_B)
    o = o_inter + jnp.einsum("bhnts,bhnse->bhnte", a_intra, v)
    return o.transpose(0, 2, 3, 1, 4).reshape(B, S, H_B, DV_B).astype(out_dtype)


def causal_dwconv(a, w, bias):
    y = lax.conv_general_dilated(a, w[:, None, :].astype(a.dtype), window_strides=(1,),
                                 padding=[(CONV_W - 1, 0)], dimension_numbers=("NWC", "WIO", "NWC"),
                                 feature_group_count=a.shape[-1])
    return y + bias


def setup_inputs(seed: int = 0) -> dict:
    key = jax.random.key(seed)
    ks = jax.random.split(key, 17)
    nrm = lambda k, shape, scale: jax.random.normal(k, shape, jnp.float32) * scale
    return {
        "x": nrm(ks[0], (BATCH, SEQ, D_MODEL), 1.0),
        "ln_mix_g": 1.0 + nrm(ks[1], (DEPTH, D_MODEL), 0.05),
        "w_in": nrm(ks[2], (DEPTH, D_MODEL, IN_W), D_MODEL ** -0.5),
        "q_norm_g": 1.0 + nrm(ks[3], (DEPTH, DH_A), 0.05),
        "k_norm_g": 1.0 + nrm(ks[4], (DEPTH, DH_A), 0.05),
        "sinks": nrm(ks[5], (DEPTH, HQ_A), 1.0),
        "w_gk_up": nrm(ks[6], (DEPTH, GK_RANK, QKB_W), GK_RANK ** -0.5),
        "b_gk": nrm(ks[7], (DEPTH, QKB_W), 0.1),
        "gla_norm_g": 1.0 + nrm(ks[8], (DEPTH, DV_B), 0.05),
        "w_branch_a": nrm(ks[9], (DEPTH, QA_W, D_MODEL), QA_W ** -0.5),
        "w_branch_b": nrm(ks[10], (DEPTH, VB_W, D_MODEL), VB_W ** -0.5),
        "w_out": nrm(ks[11], (DEPTH, D_MODEL, D_MODEL), D_MODEL ** -0.5),
        "ln_ffn_g": 1.0 + nrm(ks[12], (DEPTH, D_MODEL), 0.05),
        "w_ffn_in": nrm(ks[13], (DEPTH, D_MODEL, 2 * D_FF), D_MODEL ** -0.5),
        "conv_w": nrm(ks[14], (DEPTH, CONV_W, D_FF), CONV_W ** -0.5),
        "conv_b": nrm(ks[15], (DEPTH, D_FF), 0.02),
        "w_ffn_out": nrm(ks[16], (DEPTH, D_FF, D_MODEL), D_FF ** -0.5),
    }


def reference(x, ln_mix_g, w_in, q_norm_g, k_norm_g, sinks, w_gk_up, b_gk, gla_norm_g,
              w_branch_a, w_branch_b, w_out, ln_ffn_g, w_ffn_in, conv_w, conv_b, w_ffn_out):
    B, S = x.shape[0], x.shape[1]
    for l in range(DEPTH):
        h = rmsnorm(x, ln_mix_g[l])
        z = h @ w_in[l]
        qa, ka, va, qb, kb, vb, rb, gk_lr, ga, gb = split_cols(z)
        qa = rmsnorm(qa.reshape(B, S, HQ_A, DH_A), q_norm_g[l])
        ka = rmsnorm(ka.reshape(B, S, HKV_A, DH_A), k_norm_g[l])
        ya = sliding_window_sink_attention(qa, ka, va.reshape(B, S, HKV_A, DH_A), sinks[l])
        gk = jax.nn.log_sigmoid((gk_lr @ w_gk_up[l] + b_gk[l]).astype(jnp.float32)) / GATE_NORMALIZER
        yb = gla_chunked(qb.reshape(B, S, H_B, DK_B), kb.reshape(B, S, H_B, DK_B),
                         vb.reshape(B, S, H_B, DV_B), gk.reshape(B, S, H_B, DK_B))
        yb = rmsnorm(yb, gla_norm_g[l]) * jax.nn.silu(rb.reshape(B, S, H_B, DV_B))
        yb = yb.reshape(B, S, VB_W)
        merged = jax.nn.sigmoid(ga) * (ya @ w_branch_a[l]) + jax.nn.sigmoid(gb) * (yb @ w_branch_b[l])
        x = x + merged @ w_out[l]
        h = rmsnorm(x, ln_ffn_g[l])
        a, u = jnp.split(h @ w_ffn_in[l], 2, axis=-1)
        a = causal_dwconv(a, conv_w[l], conv_b[l])
        x = x + (jax.nn.silu(a) * u) @ w_ffn_out[l]
    return x
```

```python
import functools

import numpy as np
import jax
import jax.numpy as jnp
from jax import lax
from jax.experimental import pallas as pl
from jax.experimental.pallas import tpu as pltpu

F32 = jnp.float32
BF16 = jnp.bfloat16

D_MODEL = 1024
DEPTH = 2
HQ_A, HKV_A, DH_A = 8, 2, 64
G_A = HQ_A // HKV_A
WINDOW = 128
BLOCK_A = 128
H_B, DK_B, DV_B = 4, 128, 256
GK_RANK = 16
GATE_NORMALIZER = 16.0
CHUNK_B = 64
D_FF = 2816
CONV_W = 3
EPS = 1e-6

QA_W = HQ_A * DH_A
KVA_W = HKV_A * DH_A
QKB_W = H_B * DK_B
VB_W = H_B * DV_B
LANES = 128
_C_QA = 0
_C_KA = _C_QA + QA_W
_C_VA = _C_KA + KVA_W
_C_QB = _C_VA + KVA_W
_C_KB = _C_QB + QKB_W
_C_VB = _C_KB + QKB_W
_C_RB = _C_VB + VB_W
_C_GA = _C_RB + VB_W
_C_GB = _C_GA + D_MODEL
_C_LR = _C_GB + D_MODEL
IN_W_PAD = _C_LR + LANES
_SRC_LR = QA_W + 2 * KVA_W + 2 * QKB_W + 2 * VB_W

VMEM_LIMIT = 56 * 1024 * 1024

TM_IN = 512
TT_GLA = 512
TM_FFN = 256
N_LEVELS = 6


def _const_spec(shape):
    nd = len(shape)
    return pl.BlockSpec(shape, lambda *_: (0,) * nd, pipeline_mode=pl.Buffered(1))


def _in_proj_kernel(x_ref, g_ref, w_ref, bd_ref, qg_ref, kg_ref, wup_ref, bgk_ref,
                    qa_ref, ka_ref, va_ref, qb_ref, kb_ref, vb_ref, rb_ref, ga_ref, gb_ref, gk_ref):
    x = x_ref[...]
    ms = jnp.mean(x * x, axis=-1, keepdims=True)
    h = (x * lax.rsqrt(ms + EPS) * g_ref[...]).astype(BF16)

    def proj(lo, width):
        return jnp.dot(h, w_ref[:, lo:lo + width], preferred_element_type=F32)

    def head_norm(z, gain, width):
        ss = jnp.dot((z * z).astype(BF16), bd_ref[:width, :width], preferred_element_type=F32)
        return z * lax.rsqrt(ss * (1.0 / DH_A) + EPS) * gain

    qa_ref[...] = head_norm(proj(_C_QA, QA_W), qg_ref[...], QA_W).astype(BF16)
    ka_ref[...] = head_norm(proj(_C_KA, KVA_W), kg_ref[...], KVA_W).astype(BF16)
    va_ref[...] = proj(_C_VA, KVA_W).astype(BF16)
    qb_ref[...] = (proj(_C_QB, QKB_W) * (DK_B ** -0.5)).astype(BF16)
    kb_ref[...] = proj(_C_KB, QKB_W).astype(BF16)
    vb_ref[...] = proj(_C_VB, VB_W).astype(BF16)
    rb_ref[...] = proj(_C_RB, VB_W).astype(BF16)
    ga_ref[...] = jax.nn.sigmoid(proj(_C_GA, D_MODEL)).astype(BF16)
    gb_ref[...] = jax.nn.sigmoid(proj(_C_GB, D_MODEL)).astype(BF16)
    lr = proj(_C_LR, LANES).astype(BF16)
    t = jnp.dot(lr, wup_ref[...], preferred_element_type=F32) + bgk_ref[...]
    log_sig = jnp.minimum(t, 0.0) - jnp.log1p(jnp.exp(-jnp.abs(t)))
    gk_ref[...] = log_sig * (1.0 / GATE_NORMALIZER)


def _in_proj(x, g, w, bd, qg, kg, wup, bgk):
    T = x.shape[0]
    row = lambda width: pl.BlockSpec((TM_IN, width), lambda i: (i, 0))
    widths = (QA_W, KVA_W, KVA_W, QKB_W, QKB_W, VB_W, VB_W, D_MODEL, D_MODEL)
    out_shape = [jax.ShapeDtypeStruct((T, wd), BF16) for wd in widths] + [jax.ShapeDtypeStruct((T, QKB_W), F32)]
    out_specs = [row(wd) for wd in widths] + [row(QKB_W)]
    return pl.pallas_call(
        _in_proj_kernel,
        grid=(T // TM_IN,),
        in_specs=[row(D_MODEL), _const_spec(g.shape), _const_spec(w.shape), _const_spec(bd.shape),
                  _const_spec(qg.shape), _const_spec(kg.shape), _const_spec(wup.shape), _const_spec(bgk.shape)],
        out_specs=out_specs,
        out_shape=out_shape,
        compiler_params=pltpu.CompilerParams(dimension_semantics=("arbitrary",), vmem_limit_bytes=VMEM_LIMIT),
        name="in_proj",
    )(x, g, w, bd, qg, kg, wup, bgk)


def _swa_kernel(sinks_ref, q_ref, kp_ref, kc_ref, vp_ref, vc_ref, o_ref):
    n = pl.program_id(1)
    qpos = lax.broadcasted_iota(jnp.int32, (BLOCK_A, 2 * BLOCK_A), 0)
    kpos = lax.broadcasted_iota(jnp.int32, (BLOCK_A, 2 * BLOCK_A), 1) - BLOCK_A
    valid = (kpos <= qpos) & (kpos > qpos - WINDOW) & ((kpos >= 0) | (n > 0))
    for h in range(HKV_A):
        cs = slice(h * DH_A, (h + 1) * DH_A)
        kk = jnp.concatenate([kp_ref[:, cs], kc_ref[:, cs]], axis=0)
        vv = jnp.concatenate([vp_ref[:, cs], vc_ref[:, cs]], axis=0)
        for g in range(G_A):
            hq = h * G_A + g
            qs = slice(hq * DH_A, (hq + 1) * DH_A)
            s = lax.dot_general(q_ref[:, qs], kk, (((1,), (1,)), ((), ())), preferred_element_type=F32)
            s = jnp.where(valid, s, -jnp.inf)
            sink = sinks_ref[hq]
            m = jnp.maximum(jnp.max(s, axis=-1, keepdims=True), sink)
            p = jnp.exp(s - m)
            den = jnp.sum(p, axis=-1, keepdims=True) + jnp.exp(sink - m)
            o = jnp.dot(p.astype(BF16), vv, preferred_element_type=F32) / den
            o_ref[:, qs] = o.astype(BF16)


def _swa(sinks, q, k, v, batch):
    T = q.shape[0]
    nb = T // batch // BLOCK_A
    cur = lambda b, n: (b * nb + n, 0)
    prev = lambda b, n: (b * nb + jnp.maximum(n - 1, 0), 0)
    return pl.pallas_call(
        _swa_kernel,
        grid=(batch, nb),
        in_specs=[pl.BlockSpec(memory_space=pltpu.SMEM),
                  pl.BlockSpec((BLOCK_A, QA_W), cur),
                  pl.BlockSpec((BLOCK_A, KVA_W), prev), pl.BlockSpec((BLOCK_A, KVA_W), cur),
                  pl.BlockSpec((BLOCK_A, KVA_W), prev), pl.BlockSpec((BLOCK_A, KVA_W), cur)],
        out_specs=pl.BlockSpec((BLOCK_A, QA_W), cur),
        out_shape=jax.ShapeDtypeStruct((T, QA_W), BF16),
        compiler_params=pltpu.CompilerParams(dimension_semantics=("arbitrary", "arbitrary"),
                                             vmem_limit_bytes=VMEM_LIMIT),
        name="swa",
    )(sinks, q, k, k, v, v)


def _gla_tables():
    C = CHUNK_B
    t = np.arange(C)[:, None]
    u = np.arange(C)[None, :]
    slabs = [(u <= t), (u > t)]
    masks = []
    for l in range(N_LEVELS):
        m = (C // 2) >> l
        base = t - (t % (2 * m))
        right = (t % (2 * m)) >= m
        sel_r = (u >= base + m) & (u <= t)
        sel_l = (u > t) & (u <= base + m - 1)
        slabs.append(np.where(right, sel_r, sel_l))
        same_pair = (t // (2 * m)) == (u // (2 * m))
        masks.append(same_pair & right & ((u % (2 * m)) < m))
    masks.append(t == u)
    table = np.concatenate(slabs, axis=0).astype(np.float32)
    table3 = np.concatenate([table, table, table], axis=1)
    return jnp.asarray(table3, BF16), jnp.asarray(np.stack(masks).astype(np.float32))


def _gla_kernel(q_ref, k_ref, v_ref, gk_ref, rb_ref, tab_ref, mask_ref, gn_ref, o_ref, state_ref, ex_ref):
    @pl.when(pl.program_id(1) == 0)
    def _():
        state_ref[...] = jnp.zeros_like(state_ref)

    nt = (((1,), (1,)), ((), ()))

    def chunk(c, carry):
        rows = pl.ds(pl.multiple_of(c * CHUNK_B, CHUNK_B), CHUNK_B)
        gk = gk_ref[rows, :]
        hi = gk.astype(BF16)
        r1 = gk - hi.astype(F32)
        mid = r1.astype(BF16)
        lo = (r1 - mid.astype(F32)).astype(BF16)
        g3 = jnp.concatenate([hi, mid, lo], axis=0)
        ex_ref[...] = jnp.exp(jnp.dot(tab_ref[...], g3, preferred_element_type=F32))
        for h in range(H_B):
            kc = slice(h * DK_B, (h + 1) * DK_B)
            vc = slice(h * DV_B, (h + 1) * DV_B)
            qh = q_ref[rows, kc].astype(F32)
            kh = k_ref[rows, kc].astype(F32)
            vh = v_ref[rows, vc]
            a = mask_ref[N_LEVELS] * lax.dot_general(qh.astype(BF16), kh.astype(BF16), nt,
                                                     preferred_element_type=F32)
            for l in range(N_LEVELS):
                e = ex_ref[(2 + l) * CHUNK_B:(3 + l) * CHUNK_B, kc]
                a = a + mask_ref[l] * lax.dot_general((qh * e).astype(BF16), (kh * e).astype(BF16), nt,
                                                      preferred_element_type=F32)
            qe = (qh * ex_ref[0:CHUNK_B, kc]).astype(BF16)
            kd = kh * ex_ref[CHUNK_B:2 * CHUNK_B, kc]
            s_prev = state_ref[h]
            o = (jnp.dot(a.astype(BF16), vh, preferred_element_type=F32)
                 + jnp.dot(qe, s_prev.astype(BF16), preferred_element_type=F32))
            decay = ex_ref[CHUNK_B - 8:CHUNK_B, kc].T[:, 7:8]
            state_ref[h] = decay * s_prev + jnp.dot(kd.T.astype(BF16), vh, preferred_element_type=F32)
            ms = jnp.mean(o * o, axis=-1, keepdims=True)
            r = rb_ref[rows, vc].astype(F32)
            y = o * lax.rsqrt(ms + EPS) * gn_ref[...] * (r * jax.nn.sigmoid(r))
            o_ref[rows, vc] = y.astype(BF16)
        return carry

    lax.fori_loop(0, TT_GLA // CHUNK_B, chunk, 0)


def _gla(q, k, v, gk, rb, tab, masks, gn, batch):
    T = q.shape[0]
    nt = T // batch // TT_GLA
    row = lambda width: pl.BlockSpec((TT_GLA, width), lambda b, i: (b * nt + i, 0))
    return pl.pallas_call(
        _gla_kernel,
        grid=(batch, nt),
        in_specs=[row(QKB_W), row(QKB_W), row(VB_W), row(QKB_W), row(VB_W),
                  _const_spec(tab.shape), _const_spec(masks.shape), _const_spec(gn.shape)],
        out_specs=row(VB_W),
        out_shape=jax.ShapeDtypeStruct((T, VB_W), BF16),
        scratch_shapes=[pltpu.VMEM((H_B, DK_B, DV_B), F32),
                        pltpu.VMEM(((2 + N_LEVELS) * CHUNK_B, QKB_W), F32)],
        compiler_params=pltpu.CompilerParams(dimension_semantics=("arbitrary", "arbitrary"),
                                             vmem_limit_bytes=VMEM_LIMIT),
        name="gla",
    )(q, k, v, gk, rb, tab, masks, gn)


HALO = 8


def _mix_ffn_kernel(tiles_per_seq, x_ref, ya_ref, yb_ref, ga_ref, gb_ref, wa_ref, wb_ref, wo_ref, g_ref,
                    wi_ref, cw_ref, cb_ref, wf_ref, o_ref, abuf_ref):
    i = pl.program_id(0)

    @pl.when(i % tiles_per_seq == 0)
    def _():
        abuf_ref[0:HALO, :] = jnp.zeros((HALO, D_FF), F32)

    merged = (ga_ref[...].astype(F32) * jnp.dot(ya_ref[...], wa_ref[...], preferred_element_type=F32)
              + gb_ref[...].astype(F32) * jnp.dot(yb_ref[...], wb_ref[...], preferred_element_type=F32))
    x1 = x_ref[...] + jnp.dot(merged.astype(BF16), wo_ref[...], preferred_element_type=F32)
    ms = jnp.mean(x1 * x1, axis=-1, keepdims=True)
    h = (x1 * lax.rsqrt(ms + EPS) * g_ref[...]).astype(BF16)
    abuf_ref[HALO:HALO + TM_FFN, :] = jnp.dot(h, wi_ref[:, 0:D_FF], preferred_element_type=F32)
    u = jnp.dot(h, wi_ref[:, D_FF:2 * D_FF], preferred_element_type=F32)
    conv = cb_ref[...]
    for j in range(CONV_W):
        lo = HALO - (CONV_W - 1) + j
        conv = conv + cw_ref[j:j + 1, :] * abuf_ref[lo:lo + TM_FFN, :]
    act = (conv * jax.nn.sigmoid(conv) * u).astype(BF16)
    abuf_ref[0:HALO, :] = abuf_ref[TM_FFN:TM_FFN + HALO, :]
    o_ref[...] = x1 + jnp.dot(act, wf_ref[...], preferred_element_type=F32)


def _mix_ffn(x, ya, yb, ga, gb, wa, wb, wo, g, wi, cw, cb, wf, batch):
    T = x.shape[0]
    row = lambda width: pl.BlockSpec((TM_FFN, width), lambda i: (i, 0))
    consts = (wa, wb, wo, g, wi, cw, cb, wf)
    return pl.pallas_call(
        functools.partial(_mix_ffn_kernel, T // batch // TM_FFN),
        grid=(T // TM_FFN,),
        in_specs=[row(D_MODEL), row(QA_W), row(VB_W), row(D_MODEL), row(D_MODEL)]
                 + [_const_spec(c.shape) for c in consts],
        out_specs=row(D_MODEL),
        out_shape=jax.ShapeDtypeStruct((T, D_MODEL), F32),
        scratch_shapes=[pltpu.VMEM((TM_FFN + HALO, D_FF), F32)],
        compiler_params=pltpu.CompilerParams(dimension_semantics=("arbitrary",), vmem_limit_bytes=VMEM_LIMIT),
        name="mix_ffn",
    )(x, ya, yb, ga, gb, *consts)


def kernel(x, ln_mix_g, w_in, q_norm_g, k_norm_g, sinks, w_gk_up, b_gk, gla_norm_g, w_branch_a, w_branch_b,
           w_out, ln_ffn_g, w_ffn_in, conv_w, conv_b, w_ffn_out):
    B, S, D = x.shape
    assert D == D_MODEL and S % TT_GLA == 0 and S % TM_IN == 0 and S % TM_FFN == 0 and S % BLOCK_A == 0
    T = B * S
    xt = x.reshape(T, D)
    head = np.arange(QA_W)[:, None] // DH_A == np.arange(QA_W)[None, :] // DH_A
    bd = jnp.asarray(head.astype(np.float32), BF16)
    tab, masks = _gla_tables()
    for l in range(DEPTH):
        w = w_in[l]
        w_pad = jnp.concatenate(
            [w[:, :_SRC_LR], w[:, _SRC_LR + GK_RANK:], w[:, _SRC_LR:_SRC_LR + GK_RANK],
             jnp.zeros((D, LANES - GK_RANK), w.dtype)], axis=1).astype(BF16)
        wup = jnp.concatenate([w_gk_up[l], jnp.zeros((LANES - GK_RANK, QKB_W), F32)], axis=0).astype(BF16)
        qg = (jnp.tile(q_norm_g[l], HQ_A) * (DH_A ** -0.5)).reshape(1, QA_W)
        kg = jnp.tile(k_norm_g[l], HKV_A).reshape(1, KVA_W)
        qa, ka, va, qb, kb, vb, rb, ga, gb, gk = _in_proj(
            xt, ln_mix_g[l].reshape(1, D), w_pad, bd, qg, kg, wup, b_gk[l].reshape(1, QKB_W))
        ya = _swa(sinks[l], qa, ka, va, B)
        yb = _gla(qb, kb, vb, gk, rb, tab, masks, gla_norm_g[l].reshape(1, DV_B), B)
        xt = _mix_ffn(xt, ya, yb, ga, gb, w_branch_a[l].astype(BF16), w_branch_b[l].astype(BF16),
                      w_out[l].astype(BF16), ln_ffn_g[l].reshape(1, D), w_ffn_in[l].astype(BF16),
                      conv_w[l], conv_b[l].reshape(1, D_FF), w_ffn_out[l].astype(BF16), B)
    return xt.reshape(B, S, D)
```

```python
import functools

import numpy as np
import jax
import jax.numpy as jnp
from jax import lax
from jax.experimental import pallas as pl
from jax.experimental.pallas import tpu as pltpu

F32 = jnp.float32
BF16 = jnp.bfloat16

D_MODEL = 1024
DEPTH = 2
HQ_A, HKV_A, DH_A = 8, 2, 64
G_A = HQ_A // HKV_A
WINDOW = 128
BLOCK_A = 128
H_B, DK_B, DV_B = 4, 128, 256
GK_RANK = 16
GATE_NORMALIZER = 16.0
CHUNK_B = 64
D_FF = 2816
CONV_W = 3
EPS = 1e-6

QA_W = HQ_A * DH_A
KVA_W = HKV_A * DH_A
QKB_W = H_B * DK_B
VB_W = H_B * DV_B
LANES = 128
MXU_N = 256
_C_QA = 0
_C_KA = _C_QA + QA_W
_C_VA = _C_KA + KVA_W
_C_QB = _C_VA + KVA_W
_C_KB = _C_QB + QKB_W
_C_VB = _C_KB + QKB_W
_C_RB = _C_VB + VB_W
_C_GA = _C_RB + VB_W
_C_GB = _C_GA + D_MODEL
_C_LR = _C_GB + D_MODEL
IN_W_PAD = _C_LR + LANES
_SRC_LR = QA_W + 2 * KVA_W + 2 * QKB_W + 2 * VB_W

VMEM_LIMIT = 56 * 1024 * 1024

TM_IN = 512
TM_FFN = 256
N_LEVELS = 6
N_BLK = TM_IN // BLOCK_A
N_CHUNK = TM_FFN // CHUNK_B
EX_ROWS = (2 + N_LEVELS) * CHUNK_B
HALO = 8
NT_DIMS = (((1,), (1,)), ((), ()))
ATT_SKEW = 2
GLA_SKEW = 1


def _const_spec(shape):
    nd = len(shape)
    return pl.BlockSpec(shape, lambda *_: (0,) * nd, pipeline_mode=pl.Buffered(1))


def _weave(main, filler):
    owed = 0.0
    for amount in main:
        owed += amount
        while owed >= 1.0:
            owed -= 1.0
            next(filler, None)
    for _ in filler:
        pass


def _in_proj_swa_kernel(tiles_per_seq, sinks_ref, x_ref, g_ref, w_ref, bd_ref, qg_ref, kg_ref, wup_ref, bgk_ref,
                        ya_ref, qb_ref, kb_ref, vb_ref, rb_ref, ga_ref, gb_ref, gk_ref, h_s, q_s, k_s, v_s):
    i = pl.program_id(0)
    first = (i % tiles_per_seq) == 0

    @pl.when(i == 0)
    def _():
        v_s[...] = jnp.ones(v_s.shape, BF16)

    @pl.when(first)
    def _():
        k_s[0:BLOCK_A, :] = jnp.zeros((BLOCK_A, KVA_W), BF16)
        v_s[0:BLOCK_A, :] = jnp.zeros((BLOCK_A, 2 * KVA_W), BF16)

    x = x_ref[...]
    ms = jnp.mean(x * x, axis=-1, keepdims=True)
    h_s[...] = (x * lax.rsqrt(ms + EPS) * g_ref[...]).astype(BF16)

    def proj(lo, width):
        return jnp.dot(h_s[...], w_ref[:, lo:lo + width], preferred_element_type=F32)

    def head_norm(z, gain, width):
        ss = jnp.dot((z * z).astype(BF16), bd_ref[:width, :width], preferred_element_type=F32)
        return z * lax.rsqrt(ss * (1.0 / DH_A) + EPS) * gain

    n_proj = (2 * QKB_W + 2 * VB_W + 2 * D_MODEL) // MXU_N + 1

    def attention():
        qpos = lax.broadcasted_iota(jnp.int32, (BLOCK_A, 2 * BLOCK_A), 0)
        kpos = lax.broadcasted_iota(jnp.int32, (BLOCK_A, 2 * BLOCK_A), 1) - BLOCK_A
        band = (kpos <= qpos) & (kpos > qpos - WINDOW)
        first_valid = band & (kpos >= jnp.where(first, 0, -BLOCK_A))

        cur = slice(BLOCK_A, BLOCK_A + TM_IN)
        zq = proj(_C_QA, QA_W)
        kv = proj(_C_KA, 2 * KVA_W)
        yield 2
        q_s[...] = head_norm(zq, qg_ref[...], QA_W).astype(BF16)
        k_s[cur, :] = head_norm(kv[:, 0:KVA_W], kg_ref[...], KVA_W).astype(BF16)
        va = kv[:, KVA_W:2 * KVA_W].astype(BF16)
        for hk in range(HKV_A):
            v_s[cur, 2 * hk * DH_A:(2 * hk + 1) * DH_A] = va[:, hk * DH_A:(hk + 1) * DH_A]
        yield 2

        def scores(j, hk, g):
            rq = slice(j * BLOCK_A, (j + 1) * BLOCK_A)
            rk = slice(j * BLOCK_A, (j + 2) * BLOCK_A)
            qs = slice((hk * G_A + g) * DH_A, (hk * G_A + g + 1) * DH_A)
            return lax.dot_general(q_s[rq, qs], k_s[rk, hk * DH_A:(hk + 1) * DH_A], NT_DIMS,
                                   preferred_element_type=F32)

        def finish(j, hk, g, s):
            rq = slice(j * BLOCK_A, (j + 1) * BLOCK_A)
            rk = slice(j * BLOCK_A, (j + 2) * BLOCK_A)
            hq = hk * G_A + g
            qs = slice(hq * DH_A, (hq + 1) * DH_A)
            s = jnp.where(first_valid if j == 0 else band, s, -jnp.inf)
            sink = sinks_ref[hq]
            m = jnp.maximum(jnp.max(s, axis=-1, keepdims=True), sink)
            p = jnp.exp(s - m).astype(BF16)
            vv = v_s[rk, 2 * hk * DH_A:(2 * hk + 2) * DH_A]
            oe = jnp.dot(p, vv, preferred_element_type=F32)
            den = oe[:, DH_A:DH_A + 1] + jnp.exp(sink - m)
            ya_ref[rq, qs] = (oe[:, 0:DH_A] / den).astype(BF16)

        per_group = (n_proj - 4) / (2 * N_BLK * HQ_A)
        pending = []
        for j in range(N_BLK):
            for hk in range(HKV_A):
                for g in range(G_A):
                    pending.append((j, hk, g, scores(j, hk, g)))
                    yield per_group
                    if len(pending) > ATT_SKEW:
                        finish(*pending.pop(0))
                        yield per_group
            if j == 1:
                k_s[0:BLOCK_A, :] = k_s[TM_IN:TM_IN + BLOCK_A, :]
                v_s[0:BLOCK_A, :] = v_s[TM_IN:TM_IN + BLOCK_A, :]
        while pending:
            finish(*pending.pop(0))
            yield per_group

    def projections():
        lr = proj(_C_LR, LANES).astype(BF16)
        t = jnp.dot(lr, wup_ref[...], preferred_element_type=F32) + bgk_ref[...]
        log_sig = jnp.minimum(t, 0.0) - jnp.log1p(jnp.exp(-jnp.abs(t)))
        gk_ref[...] = log_sig * (1.0 / GATE_NORMALIZER)
        yield
        plan = ((_C_GA, D_MODEL, ga_ref, jax.nn.sigmoid), (_C_GB, D_MODEL, gb_ref, jax.nn.sigmoid),
                (_C_QB, QKB_W, qb_ref, lambda z: z * (DK_B ** -0.5)), (_C_KB, QKB_W, kb_ref, None),
                (_C_VB, VB_W, vb_ref, None), (_C_RB, VB_W, rb_ref, None))
        for lo, width, ref, fn in plan:
            for n in range(width // MXU_N):
                z = proj(lo + n * MXU_N, MXU_N)
                ref[:, n * MXU_N:(n + 1) * MXU_N] = (z if fn is None else fn(z)).astype(BF16)
                yield

    _weave(attention(), projections())


def _in_proj_swa(sinks, x, g, w, bd, qg, kg, wup, bgk, batch):
    T = x.shape[0]
    row = lambda width: pl.BlockSpec((TM_IN, width), lambda i: (i, 0))
    widths = (QA_W, QKB_W, QKB_W, VB_W, VB_W, D_MODEL, D_MODEL)
    out_shape = [jax.ShapeDtypeStruct((T, wd), BF16) for wd in widths] + [jax.ShapeDtypeStruct((T, QKB_W), F32)]
    out_specs = [row(wd) for wd in widths] + [row(QKB_W)]
    return pl.pallas_call(
        functools.partial(_in_proj_swa_kernel, T // batch // TM_IN),
        grid=(T // TM_IN,),
        in_specs=[pl.BlockSpec(memory_space=pltpu.SMEM), row(D_MODEL), _const_spec(g.shape), _const_spec(w.shape),
                  _const_spec(bd.shape), _const_spec(qg.shape), _const_spec(kg.shape), _const_spec(wup.shape),
                  _const_spec(bgk.shape)],
        out_specs=out_specs,
        out_shape=out_shape,
        scratch_shapes=[pltpu.VMEM((TM_IN, D_MODEL), BF16),
                        pltpu.VMEM((TM_IN, QA_W), BF16),
                        pltpu.VMEM((BLOCK_A + TM_IN, KVA_W), BF16),
                        pltpu.VMEM((BLOCK_A + TM_IN, 2 * KVA_W), BF16)],
        compiler_params=pltpu.CompilerParams(dimension_semantics=("arbitrary",), vmem_limit_bytes=VMEM_LIMIT),
        name="in_proj_swa",
    )(sinks, x, g, w, bd, qg, kg, wup, bgk)


def _gla_tables():
    C = CHUNK_B
    t = np.arange(C)[:, None]
    u = np.arange(C)[None, :]
    slabs = [(u <= t), (u > t)]
    masks = []
    for l in range(N_LEVELS):
        m = (C // 2) >> l
        base = t - (t % (2 * m))
        right = (t % (2 * m)) >= m
        sel_r = (u >= base + m) & (u <= t)
        sel_l = (u > t) & (u <= base + m - 1)
        slabs.append(np.where(right, sel_r, sel_l))
        same_pair = (t // (2 * m)) == (u // (2 * m))
        masks.append(same_pair & right & ((u % (2 * m)) < m))
    masks.append(t == u)
    table = np.concatenate(slabs, axis=0).astype(np.float32)
    table3 = np.concatenate([table, table, table], axis=1)
    return jnp.asarray(table3, BF16), jnp.asarray(np.stack(masks).astype(np.float32))


def _gla_mix_ffn_kernel(tiles_per_seq, n_tiles,
                        q_ref, k_ref, v_ref, gk_ref, rb_ref, tab_ref, mask_ref, gn_ref,
                        x_ref, ya_ref, ga_ref, gb_ref, wa_ref, wb_ref, wo_ref, g_ref, wi_ref, cw_ref, cb_ref, wf_ref,
                        o_ref, state_ref, ex_ref, yb_ref, abuf_ref, mg_s, x1_s, hn_s, act_s):
    i = pl.program_id(0)
    gla_tile = jnp.minimum(i, n_tiles - 1)
    ffn_tile = jnp.maximum(i - 1, 0)
    slot_w = i % 2
    slot_r = 1 - slot_w

    @pl.when(i == 0)
    def _():
        yb_ref[1] = jnp.zeros((TM_FFN, VB_W), BF16)

    @pl.when(gla_tile % tiles_per_seq == 0)
    def _():
        state_ref[...] = jnp.zeros_like(state_ref)

    @pl.when(ffn_tile % tiles_per_seq == 0)
    def _():
        abuf_ref[0:HALO, :] = jnp.zeros((HALO, D_FF), F32)

    n_ffn = 3 * (D_MODEL // MXU_N) + 2 * (D_FF // MXU_N) + D_MODEL // MXU_N

    def gla():
        def exponents(c):
            rows = slice(c * CHUNK_B, (c + 1) * CHUNK_B)
            gk = gk_ref[rows, :]
            hi = gk.astype(BF16)
            r1 = gk - hi.astype(F32)
            mid = r1.astype(BF16)
            lo = (r1 - mid.astype(F32)).astype(BF16)
            g3 = jnp.concatenate([hi, mid, lo], axis=0)
            ex_ref[c % 2] = jnp.exp(jnp.dot(tab_ref[...], g3, preferred_element_type=F32))

        def prep(c, h):
            rows = slice(c * CHUNK_B, (c + 1) * CHUNK_B)
            kc = slice(h * DK_B, (h + 1) * DK_B)
            ex = ex_ref.at[c % 2]
            qb16 = q_ref[rows, kc]
            kb16 = k_ref[rows, kc]
            qh = qb16.astype(F32)
            kh = kb16.astype(F32)
            pairs = []
            for l in range(N_LEVELS):
                e = ex[(2 + l) * CHUNK_B:(3 + l) * CHUNK_B, kc]
                pairs.append(((qh * e).astype(BF16), (kh * e).astype(BF16)))
            pairs.append((qb16, kb16))
            qe = (qh * ex[0:CHUNK_B, kc]).astype(BF16)
            kdt = (kh * ex[CHUNK_B:2 * CHUNK_B, kc]).T.astype(BF16)
            return pairs, qe, kdt

        def levels(pairs):
            return [lax.dot_general(ql, kl, NT_DIMS, preferred_element_type=F32) for ql, kl in pairs]

        def outputs(c, h, raw, qe, kdt):
            rows = slice(c * CHUNK_B, (c + 1) * CHUNK_B)
            kc = slice(h * DK_B, (h + 1) * DK_B)
            vc = slice(h * DV_B, (h + 1) * DV_B)
            ex = ex_ref.at[c % 2]
            a = mask_ref[0] * raw[0]
            for l in range(1, N_LEVELS + 1):
                a = a + mask_ref[l] * raw[l]
            vh = v_ref[rows, vc]
            s_prev = state_ref[h]
            o = (jnp.dot(a.astype(BF16), vh, preferred_element_type=F32)
                 + jnp.dot(qe, s_prev.astype(BF16), preferred_element_type=F32))
            decay = ex[CHUNK_B - 8:CHUNK_B, kc].T[:, 7:8]
            state_ref[h] = decay * s_prev + jnp.dot(kdt, vh, preferred_element_type=F32)
            ms = jnp.mean(o * o, axis=-1, keepdims=True)
            r = rb_ref[rows, vc].astype(F32)
            y = o * lax.rsqrt(ms + EPS) * gn_ref[...] * (r * jax.nn.sigmoid(r))
            yb_ref[slot_w, rows, vc] = y.astype(BF16)

        heads = [(c, h) for c in range(N_CHUNK) for h in range(H_B)]
        per_group = n_ffn / (N_CHUNK * (1 + 2 * H_B))
        exponents(0)
        yield per_group
        prepped = [prep(*heads[0])]
        in_flight = []
        for idx, (c, h) in enumerate(heads):
            if h == 0 and c + 1 < N_CHUNK:
                exponents(c + 1)
                yield per_group
            pairs, qe, kdt = prepped.pop(0)
            if idx + 1 < len(heads):
                prepped.append(prep(*heads[idx + 1]))
            in_flight.append((c, h, levels(pairs), qe, kdt))
            yield per_group
            if len(in_flight) > GLA_SKEW:
                outputs(*in_flight.pop(0))
                yield per_group
        while in_flight:
            outputs(*in_flight.pop(0))
            yield per_group

    def ffn():
        tile = lambda n: slice(n * MXU_N, (n + 1) * MXU_N)
        for n in range(D_MODEL // MXU_N):
            pa = jnp.dot(ya_ref[...], wa_ref[:, tile(n)], preferred_element_type=F32)
            yield
            pb = jnp.dot(yb_ref[slot_r], wb_ref[:, tile(n)], preferred_element_type=F32)
            mg_s[:, tile(n)] = (ga_ref[:, tile(n)].astype(F32) * pa + gb_ref[:, tile(n)].astype(F32) * pb).astype(BF16)
            yield
        for n in range(D_MODEL // MXU_N):
            x1_s[:, tile(n)] = x_ref[:, tile(n)] + jnp.dot(mg_s[...], wo_ref[:, tile(n)], preferred_element_type=F32)
            yield
        x1 = x1_s[...]
        ms = jnp.mean(x1 * x1, axis=-1, keepdims=True)
        hn_s[...] = (x1 * lax.rsqrt(ms + EPS) * g_ref[...]).astype(BF16)
        for n in range(D_FF // MXU_N):
            abuf_ref[HALO:HALO + TM_FFN, tile(n)] = jnp.dot(hn_s[...], wi_ref[:, tile(n)], preferred_element_type=F32)
            yield
            u = jnp.dot(hn_s[...], wi_ref[:, D_FF + n * MXU_N:D_FF + (n + 1) * MXU_N], preferred_element_type=F32)
            conv = cb_ref[:, tile(n)]
            for j in range(CONV_W):
                lo_r = HALO - (CONV_W - 1) + j
                conv = conv + cw_ref[j:j + 1, tile(n)] * abuf_ref[lo_r:lo_r + TM_FFN, tile(n)]
            act_s[:, tile(n)] = (conv * jax.nn.sigmoid(conv) * u).astype(BF16)
            yield
        abuf_ref[0:HALO, :] = abuf_ref[TM_FFN:TM_FFN + HALO, :]
        for n in range(D_MODEL // MXU_N):
            o_ref[:, tile(n)] = x1_s[:, tile(n)] + jnp.dot(act_s[...], wf_ref[:, tile(n)], preferred_element_type=F32)
            yield

    _weave(gla(), ffn())


def _gla_mix_ffn(q, k, v, gk, rb, tab, masks, gn, x, ya, ga, gb, wa, wb, wo, g, wi, cw, cb, wf, batch):
    T = x.shape[0]
    n_tiles = T // TM_FFN
    gla_row = lambda width: pl.BlockSpec((TM_FFN, width), lambda i: (jnp.minimum(i, n_tiles - 1), 0))
    ffn_row = lambda width: pl.BlockSpec((TM_FFN, width), lambda i: (jnp.maximum(i - 1, 0), 0))
    consts = (wa, wb, wo, g, wi, cw, cb, wf)
    return pl.pallas_call(
        functools.partial(_gla_mix_ffn_kernel, T // batch // TM_FFN, n_tiles),
        grid=(n_tiles + 1,),
        in_specs=[gla_row(QKB_W), gla_row(QKB_W), gla_row(VB_W), gla_row(QKB_W), gla_row(VB_W),
                  _const_spec(tab.shape), _const_spec(masks.shape), _const_spec(gn.shape),
                  ffn_row(D_MODEL), ffn_row(QA_W), ffn_row(D_MODEL), ffn_row(D_MODEL)]
                 + [_const_spec(c.shape) for c in consts],
        out_specs=ffn_row(D_MODEL),
        out_shape=jax.ShapeDtypeStruct((T, D_MODEL), F32),
        scratch_shapes=[pltpu.VMEM((H_B, DK_B, DV_B), F32),
                        pltpu.VMEM((2, EX_ROWS, QKB_W), F32),
                        pltpu.VMEM((2, TM_FFN, VB_W), BF16),
                        pltpu.VMEM((TM_FFN + HALO, D_FF), F32),
                        pltpu.VMEM((TM_FFN, D_MODEL), BF16),
                        pltpu.VMEM((TM_FFN, D_MODEL), F32),
                        pltpu.VMEM((TM_FFN, D_MODEL), BF16),
                        pltpu.VMEM((TM_FFN, D_FF), BF16)],
        compiler_params=pltpu.CompilerParams(dimension_semantics=("arbitrary",), vmem_limit_bytes=VMEM_LIMIT),
        name="gla_mix_ffn",
    )(q, k, v, gk, rb, tab, masks, gn, x, ya, ga, gb, *consts)


def kernel(x, ln_mix_g, w_in, q_norm_g, k_norm_g, sinks, w_gk_up, b_gk, gla_norm_g, w_branch_a, w_branch_b,
           w_out, ln_ffn_g, w_ffn_in, conv_w, conv_b, w_ffn_out):
    B, S, D = x.shape
    assert D == D_MODEL and S % TM_IN == 0 and S % TM_FFN == 0
    T = B * S
    xt = x.reshape(T, D)
    head = np.arange(QA_W)[:, None] // DH_A == np.arange(QA_W)[None, :] // DH_A
    bd = jnp.asarray(head.astype(np.float32), BF16)
    tab, masks = _gla_tables()
    for l in range(DEPTH):
        w = w_in[l]
        w_pad = jnp.concatenate(
            [w[:, :_SRC_LR], w[:, _SRC_LR + GK_RANK:], w[:, _SRC_LR:_SRC_LR + GK_RANK],
             jnp.zeros((D, LANES - GK_RANK), w.dtype)], axis=1).astype(BF16)
        wup = jnp.concatenate([w_gk_up[l], jnp.zeros((LANES - GK_RANK, QKB_W), F32)], axis=0).astype(BF16)
        qg = (jnp.tile(q_norm_g[l], HQ_A) * (DH_A ** -0.5)).reshape(1, QA_W)
        kg = jnp.tile(k_norm_g[l], HKV_A).reshape(1, KVA_W)
        ya, qb, kb, vb, rb, ga, gb, gk = _in_proj_swa(
            sinks[l], xt, ln_mix_g[l].reshape(1, D), w_pad, bd, qg, kg, wup, b_gk[l].reshape(1, QKB_W), B)
        xt = _gla_mix_ffn(qb, kb, vb, gk, rb, tab, masks, gla_norm_g[l].reshape(1, DV_B),
                          xt, ya, ga, gb, w_branch_a[l].astype(BF16), w_branch_b[l].astype(BF16),
                          w_out[l].astype(BF16), ln_ffn_g[l].reshape(1, D), w_ffn_in[l].astype(BF16),
                          conv_w[l], conv_b[l].reshape(1, D_FF), w_ffn_out[l].astype(BF16), B)
    return xt.reshape(B, S, D)
```

```python
import functools

import numpy as np
import jax
import jax.numpy as jnp
from jax import lax
from jax.experimental import pallas as pl
from jax.experimental.pallas import tpu as pltpu

F32 = jnp.float32
BF16 = jnp.bfloat16

D_MODEL = 1024
DEPTH = 2
HQ_A, HKV_A, DH_A = 8, 2, 64
G_A = HQ_A // HKV_A
WINDOW = 128
BLOCK_A = 128
H_B, DK_B, DV_B = 4, 128, 256
GK_RANK = 16
GATE_NORMALIZER = 16.0
CHUNK_B = 64
D_FF = 2816
CONV_W = 3
EPS = 1e-6

QA_W = HQ_A * DH_A
KVA_W = HKV_A * DH_A
QKB_W = H_B * DK_B
VB_W = H_B * DV_B
LANES = 128
MXU_N = 256
IN_W = QA_W + 2 * KVA_W + 2 * QKB_W + 2 * VB_W + GK_RANK + 2 * D_MODEL
_T_QA = 0
_T_KV = _T_QA + QA_W // MXU_N
_T_QB = _T_KV + 2 * KVA_W // MXU_N
_T_KB = _T_QB + QKB_W // MXU_N
_T_VB = _T_KB + QKB_W // MXU_N
_T_RB = _T_VB + VB_W // MXU_N
_T_GATES = _T_RB + VB_W // MXU_N
N_W_TILES = -(-IN_W // MXU_N)
N_GATE_FULL = N_W_TILES - 1 - _T_GATES
GATES_W = N_GATE_FULL * MXU_N + LANES
_G_A = GK_RANK
_G_B = GK_RANK + D_MODEL
assert (_T_GATES * MXU_N, IN_W - (N_W_TILES - 1) * MXU_N) == (QA_W + 2 * KVA_W + 2 * QKB_W + 2 * VB_W, GK_RANK)

VMEM_LIMIT = 58 * 1024 * 1024

TM_IN = 512
TM_FFN = 256
N_LEVELS = 6
N_BLK = TM_IN // BLOCK_A
N_CHUNK = TM_FFN // CHUNK_B
EX_ROWS = (2 + N_LEVELS) * CHUNK_B
HALO = 8
NT_DIMS = (((1,), (1,)), ((), ()))
FF_TILES = D_FF // MXU_N
DM_TILES = D_MODEL // MXU_N
NW_IN = N_W_TILES
NW_FFN = FF_TILES * 2
ATT_SKEW = 2
GLA_SKEW = 1
PROLOGUE_FILL = (3, 2)


def _const_spec(shape):
    nd = len(shape)
    return pl.BlockSpec(shape, lambda *_: (0,) * nd, pipeline_mode=pl.Buffered(1))


def _weave(main, filler):
    owed = 0.0
    for amount in main:
        owed += amount
        while owed >= 1.0:
            owed -= 1.0
            next(filler, None)
    for _ in filler:
        pass


def _in_proj_swa_kernel(tiles_per_seq, sinks_ref, x_ref, g_ref, wblk_ref, bd_ref, qg_ref, kg_ref, wup_ref, bgk_ref,
                        ya_ref, qb_ref, kb_ref, vb_ref, rb_ref, gates_ref, gk_ref, w_s, h_s, q_s, k_s, v_s):
    step = pl.program_id(0)

    @pl.when(step < NW_IN)
    def _():
        col = lax.broadcasted_iota(jnp.int32, (D_MODEL, MXU_N), 1) + step * MXU_N
        w_s[step] = jnp.where(col < IN_W, wblk_ref[...], 0.0).astype(BF16)

    @pl.when(step >= NW_IN)
    def _():
        _in_proj_swa_compute(step - NW_IN, tiles_per_seq, sinks_ref, x_ref, g_ref, bd_ref, qg_ref, kg_ref, wup_ref,
                             bgk_ref, ya_ref, qb_ref, kb_ref, vb_ref, rb_ref, gates_ref, gk_ref,
                             w_s, h_s, q_s, k_s, v_s)


def _in_proj_swa_compute(i, tiles_per_seq, sinks_ref, x_ref, g_ref, bd_ref, qg_ref, kg_ref, wup_ref, bgk_ref,
                         ya_ref, qb_ref, kb_ref, vb_ref, rb_ref, gates_ref, gk_ref, w_s, h_s, q_s, k_s, v_s):
    first = (i % tiles_per_seq) == 0

    @pl.when(i == 0)
    def _():
        v_s[...] = jnp.ones(v_s.shape, BF16)

    @pl.when(first)
    def _():
        k_s[0:BLOCK_A, :] = jnp.zeros((BLOCK_A, KVA_W), BF16)
        v_s[0:BLOCK_A, :] = jnp.zeros((BLOCK_A, 2 * KVA_W), BF16)

    x = x_ref[...]
    ms = jnp.mean(x * x, axis=-1, keepdims=True)
    h_s[...] = (x * lax.rsqrt(ms + EPS) * g_ref[...]).astype(BF16)

    def proj(t, width=MXU_N):
        return jnp.dot(h_s[...], w_s[t][:, 0:width], preferred_element_type=F32)

    def head_norm(z, gain, width):
        ss = jnp.dot((z * z).astype(BF16), bd_ref[:width, :width], preferred_element_type=F32)
        return z * lax.rsqrt(ss * (1.0 / DH_A) + EPS) * gain

    n_proj = N_W_TILES - _T_QB

    def attention():
        qpos = lax.broadcasted_iota(jnp.int32, (BLOCK_A, 2 * BLOCK_A), 0)
        kpos = lax.broadcasted_iota(jnp.int32, (BLOCK_A, 2 * BLOCK_A), 1) - BLOCK_A
        band = (kpos <= qpos) & (kpos > qpos - WINDOW)
        first_valid = band & (kpos >= jnp.where(first, 0, -BLOCK_A))

        cur = slice(BLOCK_A, BLOCK_A + TM_IN)
        zq = jnp.concatenate([proj(_T_QA + n) for n in range(QA_W // MXU_N)], axis=1)
        kv = proj(_T_KV)
        yield PROLOGUE_FILL[0]
        q_s[...] = head_norm(zq, qg_ref[...], QA_W).astype(BF16)
        k_s[cur, :] = head_norm(kv[:, 0:KVA_W], kg_ref[...], KVA_W).astype(BF16)
        va = kv[:, KVA_W:2 * KVA_W].astype(BF16)
        for hk in range(HKV_A):
            v_s[cur, 2 * hk * DH_A:(2 * hk + 1) * DH_A] = va[:, hk * DH_A:(hk + 1) * DH_A]
        yield PROLOGUE_FILL[1]

        def scores(j, hk, g):
            rq = slice(j * BLOCK_A, (j + 1) * BLOCK_A)
            rk = slice(j * BLOCK_A, (j + 2) * BLOCK_A)
            qs = slice((hk * G_A + g) * DH_A, (hk * G_A + g + 1) * DH_A)
            return lax.dot_general(q_s[rq, qs], k_s[rk, hk * DH_A:(hk + 1) * DH_A], NT_DIMS,
                                   preferred_element_type=F32)

        def finish(j, hk, g, s):
            rq = slice(j * BLOCK_A, (j + 1) * BLOCK_A)
            rk = slice(j * BLOCK_A, (j + 2) * BLOCK_A)
            hq = hk * G_A + g
            qs = slice(hq * DH_A, (hq + 1) * DH_A)
            s = jnp.where(first_valid if j == 0 else band, s, -jnp.inf)
            sink = sinks_ref[hq]
            m = jnp.maximum(jnp.max(s, axis=-1, keepdims=True), sink)
            p = jnp.exp(s - m).astype(BF16)
            vv = v_s[rk, 2 * hk * DH_A:(2 * hk + 2) * DH_A]
            oe = jnp.dot(p, vv, preferred_element_type=F32)
            den = oe[:, DH_A:DH_A + 1] + jnp.exp(sink - m)
            ya_ref[rq, qs] = (oe[:, 0:DH_A] / den).astype(BF16)

        per_group = (n_proj - sum(PROLOGUE_FILL)) / (2 * N_BLK * HQ_A)
        pending = []
        for j in range(N_BLK):
            for hk in range(HKV_A):
                for g in range(G_A):
                    pending.append((j, hk, g, scores(j, hk, g)))
                    yield per_group
                    if len(pending) > ATT_SKEW:
                        finish(*pending.pop(0))
                        yield per_group
            if j == 1:
                k_s[0:BLOCK_A, :] = k_s[TM_IN:TM_IN + BLOCK_A, :]
                v_s[0:BLOCK_A, :] = v_s[TM_IN:TM_IN + BLOCK_A, :]
        while pending:
            finish(*pending.pop(0))
            yield per_group

    def projections():
        z = proj(_T_GATES)
        gates_ref[:, 0:MXU_N] = jax.nn.sigmoid(z).astype(BF16)
        lr = z[:, 0:LANES].astype(BF16)
        t = jnp.dot(lr, wup_ref[...], preferred_element_type=F32) + bgk_ref[...]
        log_sig = jnp.minimum(t, 0.0) - jnp.log1p(jnp.exp(-jnp.abs(t)))
        gk_ref[...] = log_sig * (1.0 / GATE_NORMALIZER)
        yield
        for n in range(1, N_GATE_FULL):
            gates_ref[:, n * MXU_N:(n + 1) * MXU_N] = jax.nn.sigmoid(proj(_T_GATES + n)).astype(BF16)
            yield
        gates_ref[:, N_GATE_FULL * MXU_N:GATES_W] = jax.nn.sigmoid(proj(N_W_TILES - 1, LANES)).astype(BF16)
        yield
        plan = ((_T_QB, QKB_W, qb_ref, lambda z: z * (DK_B ** -0.5)), (_T_KB, QKB_W, kb_ref, None),
                (_T_VB, VB_W, vb_ref, None), (_T_RB, VB_W, rb_ref, None))
        for t0, width, ref, fn in plan:
            for n in range(width // MXU_N):
                z = proj(t0 + n)
                ref[:, n * MXU_N:(n + 1) * MXU_N] = (z if fn is None else fn(z)).astype(BF16)
                yield

    _weave(attention(), projections())


def _in_proj_swa(layer, sinks, x, g, w_in, bd, qg, kg, wup, bgk, batch):
    T = x.shape[0]
    row = lambda width: pl.BlockSpec((TM_IN, width), lambda s: (jnp.maximum(s - NW_IN, 0), 0))
    widths = (QA_W, QKB_W, QKB_W, VB_W, VB_W, GATES_W)
    out_shape = [jax.ShapeDtypeStruct((T, wd), BF16) for wd in widths] + [jax.ShapeDtypeStruct((T, QKB_W), F32)]
    out_specs = [row(wd) for wd in widths] + [row(QKB_W)]
    w_tile = pl.BlockSpec((None, D_MODEL, MXU_N), lambda s: (layer, 0, jnp.minimum(s, NW_IN - 1)))
    return pl.pallas_call(
        functools.partial(_in_proj_swa_kernel, T // batch // TM_IN),
        grid=(NW_IN + T // TM_IN,),
        in_specs=[pl.BlockSpec(memory_space=pltpu.SMEM), row(D_MODEL), _const_spec(g.shape), w_tile,
                  _const_spec(bd.shape), _const_spec(qg.shape), _const_spec(kg.shape), _const_spec(wup.shape),
                  _const_spec(bgk.shape)],
        out_specs=out_specs,
        out_shape=out_shape,
        scratch_shapes=[pltpu.VMEM((N_W_TILES, D_MODEL, MXU_N), BF16),
                        pltpu.VMEM((TM_IN, D_MODEL), BF16),
                        pltpu.VMEM((TM_IN, QA_W), BF16),
                        pltpu.VMEM((BLOCK_A + TM_IN, KVA_W), BF16),
                        pltpu.VMEM((BLOCK_A + TM_IN, 2 * KVA_W), BF16)],
        compiler_params=pltpu.CompilerParams(dimension_semantics=("arbitrary",), vmem_limit_bytes=VMEM_LIMIT),
        name="in_proj_swa",
    )(sinks, x, g, w_in, bd, qg, kg, wup, bgk)


def _gla_tables():
    C = CHUNK_B
    t = np.arange(C)[:, None]
    u = np.arange(C)[None, :]
    slabs = [(u <= t), (u > t)]
    masks = []
    for l in range(N_LEVELS):
        m = (C // 2) >> l
        base = t - (t % (2 * m))
        right = (t % (2 * m)) >= m
        sel_r = (u >= base + m) & (u <= t)
        sel_l = (u > t) & (u <= base + m - 1)
        slabs.append(np.where(right, sel_r, sel_l))
        same_pair = (t // (2 * m)) == (u // (2 * m))
        masks.append(same_pair & right & ((u % (2 * m)) < m))
    masks.append(t == u)
    table = np.concatenate(slabs, axis=0).astype(np.float32)
    table3 = np.concatenate([table, table, table], axis=1)
    return jnp.asarray(table3, BF16), jnp.asarray(np.stack(masks).astype(np.float32))


def _gla_mix_ffn_kernel(tiles_per_seq, n_tiles,
                        q_ref, k_ref, v_ref, gk_ref, rb_ref, tab_ref, mask_ref, gn_ref,
                        x_ref, ya_ref, gates_ref, wa_blk, wb_blk, wo_blk, g_ref, wi_blk, cw_ref, cb_ref, wf_blk,
                        o_ref, wa_s, wb_s, wo_s, wi_s, wf_s, state_ref, ex_ref, yb_ref, abuf_ref,
                        mg_s, x1_s, hn_s, act_s):
    step = pl.program_id(0)

    @pl.when(step < NW_FFN)
    def _():
        wi_s[step] = wi_blk[...].astype(BF16)

        @pl.when(step < FF_TILES)
        def _():
            wf_s[step] = wf_blk[...].astype(BF16)

        @pl.when(step < DM_TILES)
        def _():
            wa_s[step] = wa_blk[...].astype(BF16)
            wb_s[step] = wb_blk[...].astype(BF16)
            wo_s[step] = wo_blk[...].astype(BF16)

    @pl.when(step >= NW_FFN)
    def _():
        _gla_mix_ffn_compute(step - NW_FFN, tiles_per_seq, n_tiles, q_ref, k_ref, v_ref, gk_ref, rb_ref, tab_ref,
                             mask_ref, gn_ref, x_ref, ya_ref, gates_ref, g_ref, cw_ref, cb_ref, o_ref,
                             wa_s, wb_s, wo_s, wi_s, wf_s, state_ref, ex_ref, yb_ref, abuf_ref,
                             mg_s, x1_s, hn_s, act_s)


def _gla_mix_ffn_compute(i, tiles_per_seq, n_tiles, q_ref, k_ref, v_ref, gk_ref, rb_ref, tab_ref, mask_ref, gn_ref,
                         x_ref, ya_ref, gates_ref, g_ref, cw_ref, cb_ref, o_ref,
                         wa_s, wb_s, wo_s, wi_s, wf_s, state_ref, ex_ref, yb_ref, abuf_ref, mg_s, x1_s, hn_s, act_s):
    gla_tile = jnp.minimum(i, n_tiles - 1)
    ffn_tile = jnp.maximum(i - 1, 0)
    slot_w = i % 2
    slot_r = 1 - slot_w

    @pl.when(i == 0)
    def _():
        yb_ref[1] = jnp.zeros((TM_FFN, VB_W), BF16)

    @pl.when(gla_tile % tiles_per_seq == 0)
    def _():
        state_ref[...] = jnp.zeros_like(state_ref)

    @pl.when(ffn_tile % tiles_per_seq == 0)
    def _():
        abuf_ref[0:HALO, :] = jnp.zeros((HALO, D_FF), F32)

    n_ffn = 4 * DM_TILES + 2 * FF_TILES

    def gla():
        def exponents(c):
            rows = slice(c * CHUNK_B, (c + 1) * CHUNK_B)
            gk = gk_ref[rows, :]
            hi = gk.astype(BF16)
            r1 = gk - hi.astype(F32)
            mid = r1.astype(BF16)
            lo = (r1 - mid.astype(F32)).astype(BF16)
            g3 = jnp.concatenate([hi, mid, lo], axis=0)
            ex_ref[c % 2] = jnp.exp(jnp.dot(tab_ref[...], g3, preferred_element_type=F32))

        def prep(c, h):
            rows = slice(c * CHUNK_B, (c + 1) * CHUNK_B)
            kc = slice(h * DK_B, (h + 1) * DK_B)
            ex = ex_ref.at[c % 2]
            qb16 = q_ref[rows, kc]
            kb16 = k_ref[rows, kc]
            qh = qb16.astype(F32)
            kh = kb16.astype(F32)
            pairs = []
            for l in range(N_LEVELS):
                e = ex[(2 + l) * CHUNK_B:(3 + l) * CHUNK_B, kc]
                pairs.append(((qh * e).astype(BF16), (kh * e).astype(BF16)))
            pairs.append((qb16, kb16))
            qe = (qh * ex[0:CHUNK_B, kc]).astype(BF16)
            kdt = (kh * ex[CHUNK_B:2 * CHUNK_B, kc]).T.astype(BF16)
            return pairs, qe, kdt

        def levels(pairs):
            return [lax.dot_general(ql, kl, NT_DIMS, preferred_element_type=F32) for ql, kl in pairs]

        def outputs(c, h, raw, qe, kdt):
            rows = slice(c * CHUNK_B, (c + 1) * CHUNK_B)
            kc = slice(h * DK_B, (h + 1) * DK_B)
            vc = slice(h * DV_B, (h + 1) * DV_B)
            ex = ex_ref.at[c % 2]
            a = mask_ref[0] * raw[0]
            for l in range(1, N_LEVELS + 1):
                a = a + mask_ref[l] * raw[l]
            vh = v_ref[rows, vc]
            s_prev = state_ref[h]
            o = (jnp.dot(a.astype(BF16), vh, preferred_element_type=F32)
                 + jnp.dot(qe, s_prev.astype(BF16), preferred_element_type=F32))
            decay = ex[CHUNK_B - 8:CHUNK_B, kc].T[:, 7:8]
            state_ref[h] = decay * s_prev + jnp.dot(kdt, vh, preferred_element_type=F32)
            ms = jnp.mean(o * o, axis=-1, keepdims=True)
            r = rb_ref[rows, vc].astype(F32)
            y = o * lax.rsqrt(ms + EPS) * gn_ref[...] * (r * jax.nn.sigmoid(r))
            yb_ref[slot_w, rows, vc] = y.astype(BF16)

        heads = [(c, h) for c in range(N_CHUNK) for h in range(H_B)]
        per_group = n_ffn / (N_CHUNK * (1 + 2 * H_B))
        exponents(0)
        yield per_group
        prepped = [prep(*heads[0])]
        in_flight = []
        for idx, (c, h) in enumerate(heads):
            if h == 0 and c + 1 < N_CHUNK:
                exponents(c + 1)
                yield per_group
            pairs, qe, kdt = prepped.pop(0)
            if idx + 1 < len(heads):
                prepped.append(prep(*heads[idx + 1]))
            in_flight.append((c, h, levels(pairs), qe, kdt))
            yield per_group
            if len(in_flight) > GLA_SKEW:
                outputs(*in_flight.pop(0))
                yield per_group
        while in_flight:
            outputs(*in_flight.pop(0))
            yield per_group

    def ffn():
        tile = lambda n: slice(n * MXU_N, (n + 1) * MXU_N)
        gate = lambda off, n: gates_ref[:, off + n * MXU_N:off + (n + 1) * MXU_N].astype(F32)
        for n in range(DM_TILES):
            pa = jnp.dot(ya_ref[...], wa_s[n], preferred_element_type=F32)
            yield
            pb = jnp.dot(yb_ref[slot_r], wb_s[n], preferred_element_type=F32)
            mg_s[:, tile(n)] = (gate(_G_A, n) * pa + gate(_G_B, n) * pb).astype(BF16)
            yield
        for n in range(DM_TILES):
            x1_s[:, tile(n)] = x_ref[:, tile(n)] + jnp.dot(mg_s[...], wo_s[n], preferred_element_type=F32)
            yield
        x1 = x1_s[...]
        ms = jnp.mean(x1 * x1, axis=-1, keepdims=True)
        hn_s[...] = (x1 * lax.rsqrt(ms + EPS) * g_ref[...]).astype(BF16)
        for n in range(FF_TILES):
            abuf_ref[HALO:HALO + TM_FFN, tile(n)] = jnp.dot(hn_s[...], wi_s[n], preferred_element_type=F32)
            yield
            u = jnp.dot(hn_s[...], wi_s[FF_TILES + n], preferred_element_type=F32)
            conv = cb_ref[:, tile(n)]
            for j in range(CONV_W):
                lo_r = HALO - (CONV_W - 1) + j
                conv = conv + cw_ref[j:j + 1, tile(n)] * abuf_ref[lo_r:lo_r + TM_FFN, tile(n)]
            act_s[:, tile(n)] = (conv * jax.nn.sigmoid(conv) * u).astype(BF16)
            yield
        abuf_ref[0:HALO, :] = abuf_ref[TM_FFN:TM_FFN + HALO, :]
        for n in range(DM_TILES):
            acc = x1_s[:, tile(n)]
            for kt in range(FF_TILES):
                acc = acc + jnp.dot(act_s[:, tile(kt)], wf_s[kt, :, tile(n)], preferred_element_type=F32)
            o_ref[:, tile(n)] = acc
            yield

    _weave(gla(), ffn())


def _gla_mix_ffn(layer, q, k, v, gk, rb, tab, masks, gn, x, ya, gates, wa, wb, wo, g, wi, cw, cb, wf, batch):
    T = x.shape[0]
    n_tiles = T // TM_FFN
    gla_row = lambda width: pl.BlockSpec((TM_FFN, width), lambda s: (jnp.clip(s - NW_FFN, 0, n_tiles - 1), 0))
    ffn_row = lambda width: pl.BlockSpec((TM_FFN, width), lambda s: (jnp.clip(s - NW_FFN - 1, 0, n_tiles - 1), 0))
    col_tile = lambda rows, n: pl.BlockSpec((None, rows, MXU_N), lambda s: (layer, 0, jnp.minimum(s, n - 1)))
    row_tile = lambda cols, n: pl.BlockSpec((None, MXU_N, cols), lambda s: (layer, jnp.minimum(s, n - 1), 0))
    return pl.pallas_call(
        functools.partial(_gla_mix_ffn_kernel, T // batch // TM_FFN, n_tiles),
        grid=(NW_FFN + n_tiles + 1,),
        in_specs=[gla_row(QKB_W), gla_row(QKB_W), gla_row(VB_W), gla_row(QKB_W), gla_row(VB_W),
                  _const_spec(tab.shape), _const_spec(masks.shape), _const_spec(gn.shape),
                  ffn_row(D_MODEL), ffn_row(QA_W), ffn_row(GATES_W),
                  col_tile(QA_W, DM_TILES), col_tile(VB_W, DM_TILES), col_tile(D_MODEL, DM_TILES),
                  _const_spec(g.shape), col_tile(D_MODEL, 2 * FF_TILES), _const_spec(cw.shape), _const_spec(cb.shape),
                  row_tile(D_MODEL, FF_TILES)],
        out_specs=ffn_row(D_MODEL),
        out_shape=jax.ShapeDtypeStruct((T, D_MODEL), F32),
        scratch_shapes=[pltpu.VMEM((DM_TILES, QA_W, MXU_N), BF16),
                        pltpu.VMEM((DM_TILES, VB_W, MXU_N), BF16),
                        pltpu.VMEM((DM_TILES, D_MODEL, MXU_N), BF16),
                        pltpu.VMEM((2 * FF_TILES, D_MODEL, MXU_N), BF16),
                        pltpu.VMEM((FF_TILES, MXU_N, D_MODEL), BF16),
                        pltpu.VMEM((H_B, DK_B, DV_B), F32),
                        pltpu.VMEM((2, EX_ROWS, QKB_W), F32),
                        pltpu.VMEM((2, TM_FFN, VB_W), BF16),
                        pltpu.VMEM((TM_FFN + HALO, D_FF), F32),
                        pltpu.VMEM((TM_FFN, D_MODEL), BF16),
                        pltpu.VMEM((TM_FFN, D_MODEL), F32),
                        pltpu.VMEM((TM_FFN, D_MODEL), BF16),
                        pltpu.VMEM((TM_FFN, D_FF), BF16)],
        compiler_params=pltpu.CompilerParams(dimension_semantics=("arbitrary",), vmem_limit_bytes=VMEM_LIMIT),
        name="gla_mix_ffn",
    )(q, k, v, gk, rb, tab, masks, gn, x, ya, gates, wa, wb, wo, g, wi, cw, cb, wf)


def kernel(x, ln_mix_g, w_in, q_norm_g, k_norm_g, sinks, w_gk_up, b_gk, gla_norm_g, w_branch_a, w_branch_b,
           w_out, ln_ffn_g, w_ffn_in, conv_w, conv_b, w_ffn_out):
    B, S, D = x.shape
    assert D == D_MODEL and S % TM_IN == 0 and S % TM_FFN == 0
    T = B * S
    xt = x.reshape(T, D)
    head = np.arange(QA_W)[:, None] // DH_A == np.arange(QA_W)[None, :] // DH_A
    bd = jnp.asarray(head.astype(np.float32), BF16)
    tab, masks = _gla_tables()
    wup = jnp.concatenate([w_gk_up, jnp.zeros((DEPTH, LANES - GK_RANK, QKB_W), F32)], axis=1).astype(BF16)
    for l in range(DEPTH):
        qg = (jnp.tile(q_norm_g[l], HQ_A) * (DH_A ** -0.5)).reshape(1, QA_W)
        kg = jnp.tile(k_norm_g[l], HKV_A).reshape(1, KVA_W)
        ya, qb, kb, vb, rb, gates, gk = _in_proj_swa(
            l, sinks[l], xt, ln_mix_g[l].reshape(1, D), w_in, bd, qg, kg, wup[l], b_gk[l].reshape(1, QKB_W), B)
        xt = _gla_mix_ffn(l, qb, kb, vb, gk, rb, tab, masks, gla_norm_g[l].reshape(1, DV_B),
                          xt, ya, gates, w_branch_a, w_branch_b, w_out, ln_ffn_g[l].reshape(1, D), w_ffn_in,
                          conv_w[l], conv_b[l].reshape(1, D_FF), w_ffn_out, B)
    return xt.reshape(B, S, D)
```

```python
import functools

import numpy as np
import jax
import jax.numpy as jnp
from jax import lax
from jax.experimental import pallas as pl
from jax.experimental.pallas import tpu as pltpu

F32 = jnp.float32
BF16 = jnp.bfloat16

D_MODEL = 1024
DEPTH = 2
HQ_A, HKV_A, DH_A = 8, 2, 64
G_A = HQ_A // HKV_A
WINDOW = 128
BLOCK_A = 128
H_B, DK_B, DV_B = 4, 128, 256
GK_RANK = 16
GATE_NORMALIZER = 16.0
CHUNK_B = 64
D_FF = 2816
CONV_W = 3
EPS = 1e-6

QA_W = HQ_A * DH_A
KVA_W = HKV_A * DH_A
QKB_W = H_B * DK_B
VB_W = H_B * DV_B
LANES = 128
MXU_N = 256
IN_W = QA_W + 2 * KVA_W + 2 * QKB_W + 2 * VB_W + GK_RANK + 2 * D_MODEL
_T_QA = 0
_T_KV = _T_QA + QA_W // MXU_N
_T_QB = _T_KV + 2 * KVA_W // MXU_N
_T_KB = _T_QB + QKB_W // MXU_N
_T_VB = _T_KB + QKB_W // MXU_N
_T_RB = _T_VB + VB_W // MXU_N
_T_GATES = _T_RB + VB_W // MXU_N
N_W_TILES = -(-IN_W // MXU_N)
N_GATE_FULL = N_W_TILES - 1 - _T_GATES
GATES_W = N_GATE_FULL * MXU_N + LANES
_G_A = GK_RANK
_G_B = GK_RANK + D_MODEL
assert (_T_GATES * MXU_N, IN_W - (N_W_TILES - 1) * MXU_N) == (QA_W + 2 * KVA_W + 2 * QKB_W + 2 * VB_W, GK_RANK)

VMEM_LIMIT = 58 * 1024 * 1024

TM_IN = 512
TM_FFN = 256
N_LEVELS = 6
N_BLK = TM_IN // BLOCK_A
N_CHUNK = TM_FFN // CHUNK_B
EX_ROWS = (2 + N_LEVELS) * CHUNK_B
HALO = 8
NT_DIMS = (((1,), (1,)), ((), ()))
FF_TILES = D_FF // MXU_N
DM_TILES = D_MODEL // MXU_N
NW_IN = N_W_TILES
NW_FFN = FF_TILES * 2
ATT_SKEW = 2
GLA_SKEW = 1
PROLOGUE_FILL = (3, 2)


def _const_spec(shape):
    nd = len(shape)
    return pl.BlockSpec(shape, lambda *_: (0,) * nd, pipeline_mode=pl.Buffered(1))


def _weave(main, filler):
    owed = 0.0
    for amount in main:
        owed += amount
        while owed >= 1.0:
            owed -= 1.0
            next(filler, None)
    for _ in filler:
        pass


def _in_proj_swa_kernel(tiles_per_seq, sinks_ref, x_ref, g_ref, wblk_ref, bd_ref, qg_ref, kg_ref, wup_ref, bgk_ref,
                        ya_ref, qb_ref, kb_ref, vb_ref, rb_ref, gates_ref, gk_ref, w_s, h_s, q_s, k_s, v_s):
    step = pl.program_id(0)

    @pl.when(step < NW_IN)
    def _():
        row = lax.broadcasted_iota(jnp.int32, (MXU_N, D_MODEL), 0) + step * MXU_N
        w_s[step] = jnp.where(row < IN_W, wblk_ref[...], 0.0).T.astype(BF16)

    @pl.when(step >= NW_IN)
    def _():
        _in_proj_swa_compute(step - NW_IN, tiles_per_seq, sinks_ref, x_ref, g_ref, bd_ref, qg_ref, kg_ref, wup_ref,
                             bgk_ref, ya_ref, qb_ref, kb_ref, vb_ref, rb_ref, gates_ref, gk_ref,
                             w_s, h_s, q_s, k_s, v_s)


def _in_proj_swa_compute(i, tiles_per_seq, sinks_ref, x_ref, g_ref, bd_ref, qg_ref, kg_ref, wup_ref, bgk_ref,
                         ya_ref, qb_ref, kb_ref, vb_ref, rb_ref, gates_ref, gk_ref, w_s, h_s, q_s, k_s, v_s):
    first = (i % tiles_per_seq) == 0

    @pl.when(i == 0)
    def _():
        v_s[...] = jnp.ones(v_s.shape, BF16)

    @pl.when(first)
    def _():
        k_s[0:BLOCK_A, :] = jnp.zeros((BLOCK_A, KVA_W), BF16)
        v_s[0:BLOCK_A, :] = jnp.zeros((BLOCK_A, 2 * KVA_W), BF16)

    x = x_ref[...]
    ms = jnp.mean(x * x, axis=-1, keepdims=True)
    h_s[...] = (x * lax.rsqrt(ms + EPS) * g_ref[...]).astype(BF16)

    def proj(t, width=MXU_N):
        return jnp.dot(h_s[...], w_s[t][:, 0:width], preferred_element_type=F32)

    def head_norm(z, gain, width):
        ss = jnp.dot((z * z).astype(BF16), bd_ref[:width, :width], preferred_element_type=F32)
        return z * lax.rsqrt(ss * (1.0 / DH_A) + EPS) * gain

    n_proj = N_W_TILES - _T_QB

    def attention():
        qpos = lax.broadcasted_iota(jnp.int32, (BLOCK_A, 2 * BLOCK_A), 0)
        kpos = lax.broadcasted_iota(jnp.int32, (BLOCK_A, 2 * BLOCK_A), 1) - BLOCK_A
        band = (kpos <= qpos) & (kpos > qpos - WINDOW)
        first_valid = band & (kpos >= jnp.where(first, 0, -BLOCK_A))

        cur = slice(BLOCK_A, BLOCK_A + TM_IN)
        zq = jnp.concatenate([proj(_T_QA + n) for n in range(QA_W // MXU_N)], axis=1)
        kv = proj(_T_KV)
        yield PROLOGUE_FILL[0]
        q_s[...] = head_norm(zq, qg_ref[...], QA_W).astype(BF16)
        k_s[cur, :] = head_norm(kv[:, 0:KVA_W], kg_ref[...], KVA_W).astype(BF16)
        va = kv[:, KVA_W:2 * KVA_W].astype(BF16)
        for hk in range(HKV_A):
            v_s[cur, 2 * hk * DH_A:(2 * hk + 1) * DH_A] = va[:, hk * DH_A:(hk + 1) * DH_A]
        yield PROLOGUE_FILL[1]

        def scores(j, hk, g):
            rq = slice(j * BLOCK_A, (j + 1) * BLOCK_A)
            rk = slice(j * BLOCK_A, (j + 2) * BLOCK_A)
            qs = slice((hk * G_A + g) * DH_A, (hk * G_A + g + 1) * DH_A)
            return lax.dot_general(q_s[rq, qs], k_s[rk, hk * DH_A:(hk + 1) * DH_A], NT_DIMS,
                                   preferred_element_type=F32)

        def finish(j, hk, g, s):
            rq = slice(j * BLOCK_A, (j + 1) * BLOCK_A)
            rk = slice(j * BLOCK_A, (j + 2) * BLOCK_A)
            hq = hk * G_A + g
            qs = slice(hq * DH_A, (hq + 1) * DH_A)
            s = jnp.where(first_valid if j == 0 else band, s, -jnp.inf)
            sink = sinks_ref[hq]
            m = jnp.maximum(jnp.max(s, axis=-1, keepdims=True), sink)
            p = jnp.exp(s - m).astype(BF16)
            vv = v_s[rk, 2 * hk * DH_A:(2 * hk + 2) * DH_A]
            oe = jnp.dot(p, vv, preferred_element_type=F32)
            den = oe[:, DH_A:DH_A + 1] + jnp.exp(sink - m)
            ya_ref[rq, qs] = (oe[:, 0:DH_A] / den).astype(BF16)

        per_group = (n_proj - sum(PROLOGUE_FILL)) / (2 * N_BLK * HQ_A)
        pending = []
        for j in range(N_BLK):
            for hk in range(HKV_A):
                for g in range(G_A):
                    pending.append((j, hk, g, scores(j, hk, g)))
                    yield per_group
                    if len(pending) > ATT_SKEW:
                        finish(*pending.pop(0))
                        yield per_group
            if j == 1:
                k_s[0:BLOCK_A, :] = k_s[TM_IN:TM_IN + BLOCK_A, :]
                v_s[0:BLOCK_A, :] = v_s[TM_IN:TM_IN + BLOCK_A, :]
        while pending:
            finish(*pending.pop(0))
            yield per_group

    def projections():
        z = proj(_T_GATES)
        gates_ref[:, 0:MXU_N] = jax.nn.sigmoid(z).astype(BF16)
        lr = z[:, 0:LANES].astype(BF16)
        t = jnp.dot(lr, wup_ref[...], preferred_element_type=F32) + bgk_ref[...]
        log_sig = jnp.minimum(t, 0.0) - jnp.log1p(jnp.exp(-jnp.abs(t)))
        gk_ref[...] = log_sig * (1.0 / GATE_NORMALIZER)
        yield
        for n in range(1, N_GATE_FULL):
            gates_ref[:, n * MXU_N:(n + 1) * MXU_N] = jax.nn.sigmoid(proj(_T_GATES + n)).astype(BF16)
            yield
        gates_ref[:, N_GATE_FULL * MXU_N:GATES_W] = jax.nn.sigmoid(proj(N_W_TILES - 1, LANES)).astype(BF16)
        yield
        plan = ((_T_QB, QKB_W, qb_ref, lambda z: z * (DK_B ** -0.5)), (_T_KB, QKB_W, kb_ref, None),
                (_T_VB, VB_W, vb_ref, None), (_T_RB, VB_W, rb_ref, None))
        for t0, width, ref, fn in plan:
            for n in range(width // MXU_N):
                z = proj(t0 + n)
                ref[:, n * MXU_N:(n + 1) * MXU_N] = (z if fn is None else fn(z)).astype(BF16)
                yield

    _weave(attention(), projections())


def _in_proj_swa(layer, sinks, x, g, w_in, bd, qg, kg, wup, bgk, batch):
    T = x.shape[0]
    row = lambda width: pl.BlockSpec((TM_IN, width), lambda s: (jnp.maximum(s - NW_IN, 0), 0))
    widths = (QA_W, QKB_W, QKB_W, VB_W, VB_W, GATES_W)
    out_shape = [jax.ShapeDtypeStruct((T, wd), BF16) for wd in widths] + [jax.ShapeDtypeStruct((T, QKB_W), F32)]
    out_specs = [row(wd) for wd in widths] + [row(QKB_W)]
    w_tile = pl.BlockSpec((None, MXU_N, D_MODEL), lambda s: (layer, jnp.minimum(s, NW_IN - 1), 0))
    return pl.pallas_call(
        functools.partial(_in_proj_swa_kernel, T // batch // TM_IN),
        grid=(NW_IN + T // TM_IN,),
        in_specs=[pl.BlockSpec(memory_space=pltpu.SMEM), row(D_MODEL), _const_spec(g.shape), w_tile,
                  _const_spec(bd.shape), _const_spec(qg.shape), _const_spec(kg.shape), _const_spec(wup.shape),
                  _const_spec(bgk.shape)],
        out_specs=out_specs,
        out_shape=out_shape,
        scratch_shapes=[pltpu.VMEM((N_W_TILES, D_MODEL, MXU_N), BF16),
                        pltpu.VMEM((TM_IN, D_MODEL), BF16),
                        pltpu.VMEM((TM_IN, QA_W), BF16),
                        pltpu.VMEM((BLOCK_A + TM_IN, KVA_W), BF16),
                        pltpu.VMEM((BLOCK_A + TM_IN, 2 * KVA_W), BF16)],
        compiler_params=pltpu.CompilerParams(dimension_semantics=("arbitrary",), vmem_limit_bytes=VMEM_LIMIT),
        name="in_proj_swa",
    )(sinks, x, g, jnp.swapaxes(w_in, 1, 2), bd, qg, kg, wup, bgk)


N_COARSE = 3


def _gla_tables():
    C = CHUNK_B
    t = np.arange(C)[:, None]
    u = np.arange(C)[None, :]
    slabs = [(u <= t)]
    masks = []
    for l in range(N_LEVELS):
        m = (C // 2) >> l
        base = t - (t % (2 * m))
        right = (t % (2 * m)) >= m
        if l >= N_COARSE:
            sel_r = (u >= base + m) & (u <= t)
            sel_l = (u > t) & (u <= base + m - 1)
            slabs.append(np.where(right, sel_r, sel_l))
        same_pair = (t // (2 * m)) == (u // (2 * m))
        masks.append(same_pair & right & ((u % (2 * m)) < m))
    masks.append(t == u)
    table = np.concatenate(slabs, axis=0).astype(np.float32)
    table3 = np.concatenate([table, table, table], axis=1)
    return jnp.asarray(table3, BF16), jnp.asarray(np.stack(masks).astype(np.float32))


def _gla_mix_ffn_kernel(tiles_per_seq, n_tiles,
                        q_ref, k_ref, v_ref, gk_ref, rb_ref, tab_ref, mask_ref, gn_ref,
                        x_ref, ya_ref, gates_ref, wa_blk, wb_blk, wo_blk, g_ref, wi_blk, cw_ref, cb_ref, wf_blk,
                        o_ref, wa_s, wb_s, wo_s, wi_s, wf_s, state_ref, ex_ref, yb_ref, abuf_ref,
                        mg_s, x1_s, hn_s, act_s):
    step = pl.program_id(0)

    @pl.when(step < NW_FFN)
    def _():
        wi_s[step] = wi_blk[...].astype(BF16)

        @pl.when(step < FF_TILES)
        def _():
            wf_s[step] = wf_blk[...].astype(BF16)

        @pl.when(step < DM_TILES)
        def _():
            wa_s[step] = wa_blk[...].astype(BF16)
            wb_s[step] = wb_blk[...].astype(BF16)
            wo_s[step] = wo_blk[...].astype(BF16)

    @pl.when(step >= NW_FFN)
    def _():
        _gla_mix_ffn_compute(step - NW_FFN, tiles_per_seq, n_tiles, q_ref, k_ref, v_ref, gk_ref, rb_ref, tab_ref,
                             mask_ref, gn_ref, x_ref, ya_ref, gates_ref, g_ref, cw_ref, cb_ref, o_ref,
                             wa_s, wb_s, wo_s, wi_s, wf_s, state_ref, ex_ref, yb_ref, abuf_ref,
                             mg_s, x1_s, hn_s, act_s)


def _gla_mix_ffn_compute(i, tiles_per_seq, n_tiles, q_ref, k_ref, v_ref, gk_ref, rb_ref, tab_ref, mask_ref, gn_ref,
                         x_ref, ya_ref, gates_ref, g_ref, cw_ref, cb_ref, o_ref,
                         wa_s, wb_s, wo_s, wi_s, wf_s, state_ref, ex_ref, yb_ref, abuf_ref, mg_s, x1_s, hn_s, act_s):
    gla_tile = jnp.minimum(i, n_tiles - 1)
    ffn_tile = jnp.maximum(i - 1, 0)
    slot_w = i % 2
    slot_r = 1 - slot_w

    @pl.when(i == 0)
    def _():
        yb_ref[1] = jnp.zeros((TM_FFN, VB_W), BF16)

    @pl.when(gla_tile % tiles_per_seq == 0)
    def _():
        state_ref[...] = jnp.zeros_like(state_ref)

    @pl.when(ffn_tile % tiles_per_seq == 0)
    def _():
        abuf_ref[0:HALO, :] = jnp.zeros((HALO, D_FF), F32)

    n_ffn = 4 * DM_TILES + 2 * FF_TILES

    def gla():
        def exponents(c):
            rows = slice(c * CHUNK_B, (c + 1) * CHUNK_B)
            ex = ex_ref.at[c % 2]
            gk = gk_ref[rows, :]
            hi = gk.astype(BF16)
            r1 = gk - hi.astype(F32)
            mid = r1.astype(BF16)
            lo = (r1 - mid.astype(F32)).astype(BF16)
            g3 = jnp.concatenate([hi, mid, lo], axis=0)
            sums = jnp.dot(tab_ref[...], g3, preferred_element_type=F32)
            b = sums[0:CHUNK_B]
            at = lambda r, n: jnp.broadcast_to(b[r:r + 1, :], (n, QKB_W))
            ex[0:CHUNK_B, :] = jnp.exp(b)
            ex[CHUNK_B:2 * CHUNK_B, :] = jnp.exp(at(CHUNK_B - 1, CHUNK_B) - b)
            for l in range(N_COARSE):
                m = (CHUNK_B // 2) >> l
                for p in range(0, CHUNK_B, 2 * m):
                    r = at(p + m - 1, m)
                    lo_rows = slice((2 + l) * CHUNK_B + p, (2 + l) * CHUNK_B + p + m)
                    hi_rows = slice((2 + l) * CHUNK_B + p + m, (2 + l) * CHUNK_B + p + 2 * m)
                    ex[lo_rows, :] = jnp.exp(r - b[p:p + m])
                    ex[hi_rows, :] = jnp.exp(b[p + m:p + 2 * m] - r)
            ex[(2 + N_COARSE) * CHUNK_B:EX_ROWS, :] = jnp.exp(sums[CHUNK_B:])

        def prep(c, h):
            rows = slice(c * CHUNK_B, (c + 1) * CHUNK_B)
            kc = slice(h * DK_B, (h + 1) * DK_B)
            ex = ex_ref.at[c % 2]
            qb16 = q_ref[rows, kc]
            kb16 = k_ref[rows, kc]
            qh = qb16.astype(F32)
            kh = kb16.astype(F32)
            pairs = []
            for l in range(N_LEVELS):
                e = ex[(2 + l) * CHUNK_B:(3 + l) * CHUNK_B, kc]
                pairs.append(((qh * e).astype(BF16), (kh * e).astype(BF16)))
            pairs.append((qb16, kb16))
            qe = (qh * ex[0:CHUNK_B, kc]).astype(BF16)
            kdt = (kh * ex[CHUNK_B:2 * CHUNK_B, kc]).T.astype(BF16)
            return pairs, qe, kdt

        def levels(pairs):
            return [lax.dot_general(ql, kl, NT_DIMS, preferred_element_type=F32) for ql, kl in pairs]

        def outputs(c, h, raw, qe, kdt):
            rows = slice(c * CHUNK_B, (c + 1) * CHUNK_B)
            kc = slice(h * DK_B, (h + 1) * DK_B)
            vc = slice(h * DV_B, (h + 1) * DV_B)
            ex = ex_ref.at[c % 2]
            a = mask_ref[0] * raw[0]
            for l in range(1, N_LEVELS + 1):
                a = a + mask_ref[l] * raw[l]
            vh = v_ref[rows, vc]
            s_prev = state_ref[h]
            o = jnp.dot(jnp.concatenate([a.astype(BF16), qe], axis=1),
                        jnp.concatenate([vh, s_prev.astype(BF16)], axis=0), preferred_element_type=F32)
            decay = ex[CHUNK_B - 8:CHUNK_B, kc].T[:, 7:8]
            state_ref[h] = decay * s_prev + jnp.dot(kdt, vh, preferred_element_type=F32)
            ms = jnp.mean(o * o, axis=-1, keepdims=True)
            r = rb_ref[rows, vc].astype(F32)
            y = o * lax.rsqrt(ms + EPS) * gn_ref[...] * (r * jax.nn.sigmoid(r))
            yb_ref[slot_w, rows, vc] = y.astype(BF16)

        heads = [(c, h) for c in range(N_CHUNK) for h in range(H_B)]
        per_group = n_ffn / (N_CHUNK * (1 + 2 * H_B))
        exponents(0)
        yield per_group
        prepped = [prep(*heads[0])]
        in_flight = []
        for idx, (c, h) in enumerate(heads):
            if h == 0 and c + 1 < N_CHUNK:
                exponents(c + 1)
                yield per_group
            pairs, qe, kdt = prepped.pop(0)
            if idx + 1 < len(heads):
                prepped.append(prep(*heads[idx + 1]))
            in_flight.append((c, h, levels(pairs), qe, kdt))
            yield per_group
            if len(in_flight) > GLA_SKEW:
                outputs(*in_flight.pop(0))
                yield per_group
        while in_flight:
            outputs(*in_flight.pop(0))
            yield per_group

    def ffn():
        tile = lambda n: slice(n * MXU_N, (n + 1) * MXU_N)
        gate = lambda off, n: gates_ref[:, off + n * MXU_N:off + (n + 1) * MXU_N].astype(F32)
        for n in range(DM_TILES):
            pa = jnp.dot(ya_ref[...], wa_s[n], preferred_element_type=F32)
            yield
            pb = jnp.dot(yb_ref[slot_r], wb_s[n], preferred_element_type=F32)
            mg_s[:, tile(n)] = (gate(_G_A, n) * pa + gate(_G_B, n) * pb).astype(BF16)
            yield
        for n in range(DM_TILES):
            x1_s[:, tile(n)] = x_ref[:, tile(n)] + jnp.dot(mg_s[...], wo_s[n], preferred_element_type=F32)
            yield
        x1 = x1_s[...]
        ms = jnp.mean(x1 * x1, axis=-1, keepdims=True)
        hn_s[...] = (x1 * lax.rsqrt(ms + EPS) * g_ref[...]).astype(BF16)
        for n in range(FF_TILES):
            abuf_ref[HALO:HALO + TM_FFN, tile(n)] = jnp.dot(hn_s[...], wi_s[n], preferred_element_type=F32)
            yield
            u = jnp.dot(hn_s[...], wi_s[FF_TILES + n], preferred_element_type=F32)
            conv = cb_ref[:, tile(n)]
            for j in range(CONV_W):
                lo_r = HALO - (CONV_W - 1) + j
                conv = conv + cw_ref[j:j + 1, tile(n)] * abuf_ref[lo_r:lo_r + TM_FFN, tile(n)]
            act_s[:, tile(n)] = (conv * jax.nn.sigmoid(conv) * u).astype(BF16)
            yield
        abuf_ref[0:HALO, :] = abuf_ref[TM_FFN:TM_FFN + HALO, :]
        for n in range(DM_TILES):
            acc = x1_s[:, tile(n)]
            for kt in range(FF_TILES):
                acc = acc + jnp.dot(act_s[:, tile(kt)], wf_s[kt, :, tile(n)], preferred_element_type=F32)
            o_ref[:, tile(n)] = acc
            yield

    _weave(gla(), ffn())


def _gla_mix_ffn(layer, q, k, v, gk, rb, tab, masks, gn, x, ya, gates, wa, wb, wo, g, wi, cw, cb, wf, batch):
    T = x.shape[0]
    n_tiles = T // TM_FFN
    gla_row = lambda width: pl.BlockSpec((TM_FFN, width), lambda s: (jnp.clip(s - NW_FFN, 0, n_tiles - 1), 0))
    ffn_row = lambda width: pl.BlockSpec((TM_FFN, width), lambda s: (jnp.clip(s - NW_FFN - 1, 0, n_tiles - 1), 0))
    col_tile = lambda rows, n: pl.BlockSpec((None, rows, MXU_N), lambda s: (layer, 0, jnp.minimum(s, n - 1)))
    row_tile = lambda cols, n: pl.BlockSpec((None, MXU_N, cols), lambda s: (layer, jnp.minimum(s, n - 1), 0))
    return pl.pallas_call(
        functools.partial(_gla_mix_ffn_kernel, T // batch // TM_FFN, n_tiles),
        grid=(NW_FFN + n_tiles + 1,),
        in_specs=[gla_row(QKB_W), gla_row(QKB_W), gla_row(VB_W), gla_row(QKB_W), gla_row(VB_W),
                  _const_spec(tab.shape), _const_spec(masks.shape), _const_spec(gn.shape),
                  ffn_row(D_MODEL), ffn_row(QA_W), ffn_row(GATES_W),
                  col_tile(QA_W, DM_TILES), col_tile(VB_W, DM_TILES), col_tile(D_MODEL, DM_TILES),
                  _const_spec(g.shape), col_tile(D_MODEL, 2 * FF_TILES), _const_spec(cw.shape), _const_spec(cb.shape),
                  row_tile(D_MODEL, FF_TILES)],
        out_specs=ffn_row(D_MODEL),
        out_shape=jax.ShapeDtypeStruct((T, D_MODEL), F32),
        scratch_shapes=[pltpu.VMEM((DM_TILES, QA_W, MXU_N), BF16),
                        pltpu.VMEM((DM_TILES, VB_W, MXU_N), BF16),
                        pltpu.VMEM((DM_TILES, D_MODEL, MXU_N), BF16),
                        pltpu.VMEM((2 * FF_TILES, D_MODEL, MXU_N), BF16),
                        pltpu.VMEM((FF_TILES, MXU_N, D_MODEL), BF16),
                        pltpu.VMEM((H_B, DK_B, DV_B), F32),
                        pltpu.VMEM((2, EX_ROWS, QKB_W), F32),
                        pltpu.VMEM((2, TM_FFN, VB_W), BF16),
                        pltpu.VMEM((TM_FFN + HALO, D_FF), F32),
                        pltpu.VMEM((TM_FFN, D_MODEL), BF16),
                        pltpu.VMEM((TM_FFN, D_MODEL), F32),
                        pltpu.VMEM((TM_FFN, D_MODEL), BF16),
                        pltpu.VMEM((TM_FFN, D_FF), BF16)],
        compiler_params=pltpu.CompilerParams(dimension_semantics=("arbitrary",), vmem_limit_bytes=VMEM_LIMIT),
        name="gla_mix_ffn",
    )(q, k, v, gk, rb, tab, masks, gn, x, ya, gates, wa, wb, wo, g, wi, cw, cb, wf)


def kernel(x, ln_mix_g, w_in, q_norm_g, k_norm_g, sinks, w_gk_up, b_gk, gla_norm_g, w_branch_a, w_branch_b,
           w_out, ln_ffn_g, w_ffn_in, conv_w, conv_b, w_ffn_out):
    B, S, D = x.shape
    assert D == D_MODEL and S % TM_IN == 0 and S % TM_FFN == 0
    T = B * S
    xt = x.reshape(T, D)
    head = np.arange(QA_W)[:, None] // DH_A == np.arange(QA_W)[None, :] // DH_A
    bd = jnp.asarray(head.astype(np.float32), BF16)
    tab, masks = _gla_tables()
    wup = jnp.concatenate([w_gk_up, jnp.zeros((DEPTH, LANES - GK_RANK, QKB_W), F32)], axis=1).astype(BF16)
    for l in range(DEPTH):
        qg = (jnp.tile(q_norm_g[l], HQ_A) * (DH_A ** -0.5)).reshape(1, QA_W)
        kg = jnp.tile(k_norm_g[l], HKV_A).reshape(1, KVA_W)
        ya, qb, kb, vb, rb, gates, gk = _in_proj_swa(
            l, sinks[l], xt, ln_mix_g[l].reshape(1, D), w_in, bd, qg, kg, wup[l], b_gk[l].reshape(1, QKB_W), B)
        xt = _gla_mix_ffn(l, qb, kb, vb, gk, rb, tab, masks, gla_norm_g[l].reshape(1, DV_B),
                          xt, ya, gates, w_branch_a, w_branch_b, w_out, ln_ffn_g[l].reshape(1, D), w_ffn_in,
                          conv_w[l], conv_b[l].reshape(1, D_FF), w_ffn_out, B)
    return xt.reshape(B, S, D)
```

```python
import functools

import numpy as np
import jax
import jax.numpy as jnp
from jax import lax
from jax.experimental import pallas as pl
from jax.experimental.pallas import tpu as pltpu

F32 = jnp.float32
BF16 = jnp.bfloat16

D_MODEL = 1024
DEPTH = 2
HQ_A, HKV_A, DH_A = 8, 2, 64
G_A = HQ_A // HKV_A
WINDOW = 128
BLOCK_A = 128
H_B, DK_B, DV_B = 4, 128, 256
GK_RANK = 16
GATE_NORMALIZER = 16.0
CHUNK_B = 64
D_FF = 2816
CONV_W = 3
EPS = 1e-6

QA_W = HQ_A * DH_A
KVA_W = HKV_A * DH_A
QKB_W = H_B * DK_B
VB_W = H_B * DV_B
LANES = 128
MXU_N = 256
IN_W = QA_W + 2 * KVA_W + 2 * QKB_W + 2 * VB_W + GK_RANK + 2 * D_MODEL
_T_QA = 0
_T_KV = _T_QA + QA_W // MXU_N
_T_QB = _T_KV + 2 * KVA_W // MXU_N
_T_KB = _T_QB + QKB_W // MXU_N
_T_VB = _T_KB + QKB_W // MXU_N
_T_RB = _T_VB + VB_W // MXU_N
_T_GATES = _T_RB + VB_W // MXU_N
N_W_TILES = -(-IN_W // MXU_N)
N_GATE_FULL = N_W_TILES - 1 - _T_GATES
GATES_W = N_GATE_FULL * MXU_N + LANES
_G_A = GK_RANK
_G_B = GK_RANK + D_MODEL
assert (_T_GATES * MXU_N, IN_W - (N_W_TILES - 1) * MXU_N) == (QA_W + 2 * KVA_W + 2 * QKB_W + 2 * VB_W, GK_RANK)

VMEM_LIMIT = 58 * 1024 * 1024

TM_IN = 512
TM_FFN = 256
N_LEVELS = 6
N_BLK = TM_IN // BLOCK_A
N_CHUNK = TM_FFN // CHUNK_B
EX_ROWS = (2 + N_LEVELS) * CHUNK_B
HALO = 8
NT_DIMS = (((1,), (1,)), ((), ()))
FF_TILES = D_FF // MXU_N
DM_TILES = D_MODEL // MXU_N
NW_IN = N_W_TILES
NW_FFN = FF_TILES * 2
ATT_SKEW = 2
GLA_SKEW = 1
PROLOGUE_FILL = (3, 2)


def _const_spec(shape):
    nd = len(shape)
    return pl.BlockSpec(shape, lambda *_: (0,) * nd, pipeline_mode=pl.Buffered(1))


def _weave(main, filler):
    owed = 0.0
    for amount in main:
        owed += amount
        while owed >= 1.0:
            owed -= 1.0
            next(filler, None)
    for _ in filler:
        pass


def _in_proj_swa_kernel(tiles_per_seq, sinks_ref, x_ref, g_ref, wblk_ref, bd_ref, qg_ref, kg_ref, wup_ref, bgk_ref,
                        ya_ref, qb_ref, kb_ref, vb_ref, rb_ref, gates_ref, gk_ref, w_s, h_s, q_s, k_s, v_s):
    step = pl.program_id(0)

    @pl.when(step < NW_IN)
    def _():
        row = lax.broadcasted_iota(jnp.int32, (MXU_N, D_MODEL), 0) + step * MXU_N
        w_s[step] = jnp.where(row < IN_W, wblk_ref[...], 0.0).T.astype(BF16)

    @pl.when(step >= NW_IN)
    def _():
        _in_proj_swa_compute(step - NW_IN, tiles_per_seq, sinks_ref, x_ref, g_ref, bd_ref, qg_ref, kg_ref, wup_ref,
                             bgk_ref, ya_ref, qb_ref, kb_ref, vb_ref, rb_ref, gates_ref, gk_ref,
                             w_s, h_s, q_s, k_s, v_s)


def _in_proj_swa_compute(i, tiles_per_seq, sinks_ref, x_ref, g_ref, bd_ref, qg_ref, kg_ref, wup_ref, bgk_ref,
                         ya_ref, qb_ref, kb_ref, vb_ref, rb_ref, gates_ref, gk_ref, w_s, h_s, q_s, k_s, v_s):
    first = (i % tiles_per_seq) == 0

    @pl.when(i == 0)
    def _():
        v_s[...] = jnp.ones(v_s.shape, BF16)

    @pl.when(first)
    def _():
        k_s[0:BLOCK_A, :] = jnp.zeros((BLOCK_A, KVA_W), BF16)
        v_s[0:BLOCK_A, :] = jnp.zeros((BLOCK_A, 2 * KVA_W), BF16)

    x = x_ref[...]
    ms = jnp.mean(x * x, axis=-1, keepdims=True)
    h_s[...] = (x * lax.rsqrt(ms + EPS) * g_ref[...]).astype(BF16)

    def proj(t, width=MXU_N):
        return jnp.dot(h_s[...], w_s[t][:, 0:width], preferred_element_type=F32)

    def head_norm(z, gain, width):
        ss = jnp.dot((z * z).astype(BF16), bd_ref[:width, :width], preferred_element_type=F32)
        return z * lax.rsqrt(ss * (1.0 / DH_A) + EPS) * gain

    n_proj = N_W_TILES - _T_QB

    def attention():
        qpos = lax.broadcasted_iota(jnp.int32, (BLOCK_A, 2 * BLOCK_A), 0)
        kpos = lax.broadcasted_iota(jnp.int32, (BLOCK_A, 2 * BLOCK_A), 1) - BLOCK_A
        band = (kpos <= qpos) & (kpos > qpos - WINDOW)
        first_valid = band & (kpos >= jnp.where(first, 0, -BLOCK_A))

        cur = slice(BLOCK_A, BLOCK_A + TM_IN)
        zq = jnp.concatenate([proj(_T_QA + n) for n in range(QA_W // MXU_N)], axis=1)
        kv = proj(_T_KV)
        yield PROLOGUE_FILL[0]
        q_s[...] = head_norm(zq, qg_ref[...], QA_W).astype(BF16)
        k_s[cur, :] = head_norm(kv[:, 0:KVA_W], kg_ref[...], KVA_W).astype(BF16)
        va = kv[:, KVA_W:2 * KVA_W].astype(BF16)
        for hk in range(HKV_A):
            v_s[cur, 2 * hk * DH_A:(2 * hk + 1) * DH_A] = va[:, hk * DH_A:(hk + 1) * DH_A]
        yield PROLOGUE_FILL[1]

        def scores(j, hk, g):
            rq = slice(j * BLOCK_A, (j + 1) * BLOCK_A)
            rk = slice(j * BLOCK_A, (j + 2) * BLOCK_A)
            qs = slice((hk * G_A + g) * DH_A, (hk * G_A + g + 1) * DH_A)
            return lax.dot_general(q_s[rq, qs], k_s[rk, hk * DH_A:(hk + 1) * DH_A], NT_DIMS,
                                   preferred_element_type=F32)

        def finish(j, hk, g, s):
            rq = slice(j * BLOCK_A, (j + 1) * BLOCK_A)
            rk = slice(j * BLOCK_A, (j + 2) * BLOCK_A)
            hq = hk * G_A + g
            qs = slice(hq * DH_A, (hq + 1) * DH_A)
            s = jnp.where(first_valid if j == 0 else band, s, -jnp.inf)
            sink = sinks_ref[hq]
            m = jnp.maximum(jnp.max(s, axis=-1, keepdims=True), sink)
            p = jnp.exp(s - m).astype(BF16)
            vv = v_s[rk, 2 * hk * DH_A:(2 * hk + 2) * DH_A]
            oe = jnp.dot(p, vv, preferred_element_type=F32)
            den = oe[:, DH_A:DH_A + 1] + jnp.exp(sink - m)
            ya_ref[rq, qs] = (oe[:, 0:DH_A] / den).astype(BF16)

        per_group = (n_proj - sum(PROLOGUE_FILL)) / (2 * N_BLK * HQ_A)
        pending = []
        for j in range(N_BLK):
            for hk in range(HKV_A):
                for g in range(G_A):
                    pending.append((j, hk, g, scores(j, hk, g)))
                    yield per_group
                    if len(pending) > ATT_SKEW:
                        finish(*pending.pop(0))
                        yield per_group
            if j == 1:
                k_s[0:BLOCK_A, :] = k_s[TM_IN:TM_IN + BLOCK_A, :]
                v_s[0:BLOCK_A, :] = v_s[TM_IN:TM_IN + BLOCK_A, :]
        while pending:
            finish(*pending.pop(0))
            yield per_group

    def projections():
        z = proj(_T_GATES)
        gates_ref[:, 0:MXU_N] = jax.nn.sigmoid(z).astype(BF16)
        lr = z[:, 0:LANES].astype(BF16)
        t = jnp.dot(lr, wup_ref[...], preferred_element_type=F32) + bgk_ref[...]
        log_sig = jnp.minimum(t, 0.0) - jnp.log1p(jnp.exp(-jnp.abs(t)))
        gk_ref[...] = log_sig * (1.0 / GATE_NORMALIZER)
        yield
        for n in range(1, N_GATE_FULL):
            gates_ref[:, n * MXU_N:(n + 1) * MXU_N] = jax.nn.sigmoid(proj(_T_GATES + n)).astype(BF16)
            yield
        gates_ref[:, N_GATE_FULL * MXU_N:GATES_W] = jax.nn.sigmoid(proj(N_W_TILES - 1, LANES)).astype(BF16)
        yield
        plan = ((_T_QB, QKB_W, qb_ref, lambda z: z * (DK_B ** -0.5)), (_T_KB, QKB_W, kb_ref, None),
                (_T_VB, VB_W, vb_ref, None), (_T_RB, VB_W, rb_ref, None))
        for t0, width, ref, fn in plan:
            for n in range(width // MXU_N):
                z = proj(t0 + n)
                ref[:, n * MXU_N:(n + 1) * MXU_N] = (z if fn is None else fn(z)).astype(BF16)
                yield

    _weave(attention(), projections())


def _in_proj_swa(layer, sinks, x, g, w_in, bd, qg, kg, wup, bgk, batch):
    T = x.shape[0]
    row = lambda width: pl.BlockSpec((TM_IN, width), lambda s: (jnp.maximum(s - NW_IN, 0), 0))
    widths = (QA_W, QKB_W, QKB_W, VB_W, VB_W, GATES_W)
    out_shape = [jax.ShapeDtypeStruct((T, wd), BF16) for wd in widths] + [jax.ShapeDtypeStruct((T, QKB_W), F32)]
    out_specs = [row(wd) for wd in widths] + [row(QKB_W)]
    w_tile = pl.BlockSpec((None, MXU_N, D_MODEL), lambda s: (layer, jnp.minimum(s, NW_IN - 1), 0))
    return pl.pallas_call(
        functools.partial(_in_proj_swa_kernel, T // batch // TM_IN),
        grid=(NW_IN + T // TM_IN,),
        in_specs=[pl.BlockSpec(memory_space=pltpu.SMEM), row(D_MODEL), _const_spec(g.shape), w_tile,
                  _const_spec(bd.shape), _const_spec(qg.shape), _const_spec(kg.shape), _const_spec(wup.shape),
                  _const_spec(bgk.shape)],
        out_specs=out_specs,
        out_shape=out_shape,
        scratch_shapes=[pltpu.VMEM((N_W_TILES, D_MODEL, MXU_N), BF16),
                        pltpu.VMEM((TM_IN, D_MODEL), BF16),
                        pltpu.VMEM((TM_IN, QA_W), BF16),
                        pltpu.VMEM((BLOCK_A + TM_IN, KVA_W), BF16),
                        pltpu.VMEM((BLOCK_A + TM_IN, 2 * KVA_W), BF16)],
        compiler_params=pltpu.CompilerParams(dimension_semantics=("arbitrary",), vmem_limit_bytes=VMEM_LIMIT),
        name="in_proj_swa",
    )(sinks, x, g, jnp.swapaxes(w_in, 1, 2), bd, qg, kg, wup, bgk)


def _gla_tables():
    C = CHUNK_B
    t = np.arange(C)[:, None]
    u = np.arange(C)[None, :]
    slabs = [(u <= t), (u > t)]
    masks = []
    for l in range(N_LEVELS):
        m = (C // 2) >> l
        base = t - (t % (2 * m))
        right = (t % (2 * m)) >= m
        sel_r = (u >= base + m) & (u <= t)
        sel_l = (u > t) & (u <= base + m - 1)
        slabs.append(np.where(right, sel_r, sel_l))
        same_pair = (t // (2 * m)) == (u // (2 * m))
        masks.append(same_pair & right & ((u % (2 * m)) < m))
    masks.append(t == u)
    table = np.concatenate(slabs, axis=0).astype(np.float32)
    table3 = np.concatenate([table, table, table], axis=1)
    return jnp.asarray(table3, BF16), jnp.asarray(np.stack(masks).astype(np.float32))


def _gla_mix_ffn_kernel(tiles_per_seq, n_tiles,
                        q_ref, k_ref, v_ref, gk_ref, rb_ref, tab_ref, mask_ref, gn_ref,
                        x_ref, ya_ref, gates_ref, wa_blk, wb_blk, wo_blk, g_ref, wi_blk, cw_ref, cb_ref, wf_blk,
                        o_ref, wa_s, wb_s, wo_s, wi_s, wf_s, state_ref, ex_ref, yb_ref, abuf_ref,
                        mg_s, x1_s, hn_s, act_s):
    step = pl.program_id(0)

    @pl.when(step < NW_FFN)
    def _():
        wi_s[step] = wi_blk[...].astype(BF16)

        @pl.when(step < FF_TILES)
        def _():
            wf_s[step] = wf_blk[...].astype(BF16)

        @pl.when(step < DM_TILES)
        def _():
            wa_s[step] = wa_blk[...].astype(BF16)
            wb_s[step] = wb_blk[...].astype(BF16)
            wo_s[step] = wo_blk[...].astype(BF16)

    @pl.when(step >= NW_FFN)
    def _():
        _gla_mix_ffn_compute(step - NW_FFN, tiles_per_seq, n_tiles, q_ref, k_ref, v_ref, gk_ref, rb_ref, tab_ref,
                             mask_ref, gn_ref, x_ref, ya_ref, gates_ref, g_ref, cw_ref, cb_ref, o_ref,
                             wa_s, wb_s, wo_s, wi_s, wf_s, state_ref, ex_ref, yb_ref, abuf_ref,
                             mg_s, x1_s, hn_s, act_s)


def _gla_mix_ffn_compute(i, tiles_per_seq, n_tiles, q_ref, k_ref, v_ref, gk_ref, rb_ref, tab_ref, mask_ref, gn_ref,
                         x_ref, ya_ref, gates_ref, g_ref, cw_ref, cb_ref, o_ref,
                         wa_s, wb_s, wo_s, wi_s, wf_s, state_ref, ex_ref, yb_ref, abuf_ref, mg_s, x1_s, hn_s, act_s):
    gla_tile = jnp.minimum(i, n_tiles - 1)
    ffn_tile = jnp.maximum(i - 1, 0)
    slot_w = i % 2
    slot_r = 1 - slot_w

    @pl.when(i == 0)
    def _():
        yb_ref[1] = jnp.zeros((TM_FFN, VB_W), BF16)

    @pl.when(gla_tile % tiles_per_seq == 0)
    def _():
        state_ref[...] = jnp.zeros_like(state_ref)

    @pl.when(ffn_tile % tiles_per_seq == 0)
    def _():
        abuf_ref[0:HALO, :] = jnp.zeros((HALO, D_FF), F32)

    n_ffn = 4 * DM_TILES + 2 * FF_TILES

    def gla():
        def exponents(c):
            rows = slice(c * CHUNK_B, (c + 1) * CHUNK_B)
            gk = gk_ref[rows, :]
            hi = gk.astype(BF16)
            r1 = gk - hi.astype(F32)
            mid = r1.astype(BF16)
            lo = (r1 - mid.astype(F32)).astype(BF16)
            g3 = jnp.concatenate([hi, mid, lo], axis=0)
            ex_ref[c % 2] = jnp.exp(jnp.dot(tab_ref[...], g3, preferred_element_type=F32))

        def prep(c, h):
            rows = slice(c * CHUNK_B, (c + 1) * CHUNK_B)
            kc = slice(h * DK_B, (h + 1) * DK_B)
            ex = ex_ref.at[c % 2]
            qb16 = q_ref[rows, kc]
            kb16 = k_ref[rows, kc]
            qh = qb16.astype(F32)
            kh = kb16.astype(F32)
            pairs = []
            for l in range(N_LEVELS):
                e = ex[(2 + l) * CHUNK_B:(3 + l) * CHUNK_B, kc]
                pairs.append(((qh * e).astype(BF16), (kh * e).astype(BF16)))
            pairs.append((qb16, kb16))
            qe = (qh * ex[0:CHUNK_B, kc]).astype(BF16)
            kdt = (kh * ex[CHUNK_B:2 * CHUNK_B, kc]).T.astype(BF16)
            return pairs, qe, kdt

        def levels(pairs):
            return [lax.dot_general(ql, kl, NT_DIMS, preferred_element_type=F32) for ql, kl in pairs]

        def outputs(c, h, raw, qe, kdt):
            rows = slice(c * CHUNK_B, (c + 1) * CHUNK_B)
            kc = slice(h * DK_B, (h + 1) * DK_B)
            vc = slice(h * DV_B, (h + 1) * DV_B)
            ex = ex_ref.at[c % 2]
            a = mask_ref[0] * raw[0]
            for l in range(1, N_LEVELS + 1):
                a = a + mask_ref[l] * raw[l]
            vh = v_ref[rows, vc]
            s_prev = state_ref[h]
            o = (jnp.dot(a.astype(BF16), vh, preferred_element_type=F32)
                 + jnp.dot(qe, s_prev.astype(BF16), preferred_element_type=F32))
            decay = ex[CHUNK_B - 8:CHUNK_B, kc].T[:, 7:8]
            state_ref[h] = decay * s_prev + jnp.dot(kdt, vh, preferred_element_type=F32)
            ms = jnp.mean(o * o, axis=-1, keepdims=True)
            r = rb_ref[rows, vc].astype(F32)
            y = o * lax.rsqrt(ms + EPS) * gn_ref[...] * (r * jax.nn.sigmoid(r))
            yb_ref[slot_w, rows, vc] = y.astype(BF16)

        heads = [(c, h) for c in range(N_CHUNK) for h in range(H_B)]
        per_group = n_ffn / (N_CHUNK * (1 + 2 * H_B))
        exponents(0)
        yield per_group
        prepped = [prep(*heads[0])]
        in_flight = []
        for idx, (c, h) in enumerate(heads):
            if h == 0 and c + 1 < N_CHUNK:
                exponents(c + 1)
                yield per_group
            pairs, qe, kdt = prepped.pop(0)
            if idx + 1 < len(heads):
                prepped.append(prep(*heads[idx + 1]))
            in_flight.append((c, h, levels(pairs), qe, kdt))
            yield per_group
            if len(in_flight) > GLA_SKEW:
                outputs(*in_flight.pop(0))
                yield per_group
        while in_flight:
            outputs(*in_flight.pop(0))
            yield per_group

    def ffn():
        tile = lambda n: slice(n * MXU_N, (n + 1) * MXU_N)
        gate = lambda off, n: gates_ref[:, off + n * MXU_N:off + (n + 1) * MXU_N].astype(F32)
        for n in range(DM_TILES):
            pa = jnp.dot(ya_ref[...], wa_s[n], preferred_element_type=F32)
            yield
            pb = jnp.dot(yb_ref[slot_r], wb_s[n], preferred_element_type=F32)
            mg_s[:, tile(n)] = (gate(_G_A, n) * pa + gate(_G_B, n) * pb).astype(BF16)
            yield
        for n in range(DM_TILES):
            x1_s[:, tile(n)] = x_ref[:, tile(n)] + jnp.dot(mg_s[...], wo_s[n], preferred_element_type=F32)
            yield
        x1 = x1_s[...]
        ms = jnp.mean(x1 * x1, axis=-1, keepdims=True)
        hn_s[...] = (x1 * lax.rsqrt(ms + EPS) * g_ref[...]).astype(BF16)
        for n in range(FF_TILES):
            abuf_ref[HALO:HALO + TM_FFN, tile(n)] = jnp.dot(hn_s[...], wi_s[n], preferred_element_type=F32)
            yield
            u = jnp.dot(hn_s[...], wi_s[FF_TILES + n], preferred_element_type=F32)
            conv = cb_ref[:, tile(n)]
            for j in range(CONV_W):
                lo_r = HALO - (CONV_W - 1) + j
                conv = conv + cw_ref[j:j + 1, tile(n)] * abuf_ref[lo_r:lo_r + TM_FFN, tile(n)]
            act_s[:, tile(n)] = (conv * jax.nn.sigmoid(conv) * u).astype(BF16)
            yield
        abuf_ref[0:HALO, :] = abuf_ref[TM_FFN:TM_FFN + HALO, :]
        for n in range(DM_TILES):
            acc = x1_s[:, tile(n)]
            for kt in range(FF_TILES):
                acc = acc + jnp.dot(act_s[:, tile(kt)], wf_s[kt, :, tile(n)], preferred_element_type=F32)
            o_ref[:, tile(n)] = acc
            yield

    _weave(gla(), ffn())


def _gla_mix_ffn(layer, q, k, v, gk, rb, tab, masks, gn, x, ya, gates, wa, wb, wo, g, wi, cw, cb, wf, batch):
    T = x.shape[0]
    n_tiles = T // TM_FFN
    gla_row = lambda width: pl.BlockSpec((TM_FFN, width), lambda s: (jnp.clip(s - NW_FFN, 0, n_tiles - 1), 0))
    ffn_row = lambda width: pl.BlockSpec((TM_FFN, width), lambda s: (jnp.clip(s - NW_FFN - 1, 0, n_tiles - 1), 0))
    col_tile = lambda rows, n: pl.BlockSpec((None, rows, MXU_N), lambda s: (layer, 0, jnp.minimum(s, n - 1)))
    row_tile = lambda cols, n: pl.BlockSpec((None, MXU_N, cols), lambda s: (layer, jnp.minimum(s, n - 1), 0))
    return pl.pallas_call(
        functools.partial(_gla_mix_ffn_kernel, T // batch // TM_FFN, n_tiles),
        grid=(NW_FFN + n_tiles + 1,),
        in_specs=[gla_row(QKB_W), gla_row(QKB_W), gla_row(VB_W), gla_row(QKB_W), gla_row(VB_W),
                  _const_spec(tab.shape), _const_spec(masks.shape), _const_spec(gn.shape),
                  ffn_row(D_MODEL), ffn_row(QA_W), ffn_row(GATES_W),
                  col_tile(QA_W, DM_TILES), col_tile(VB_W, DM_TILES), col_tile(D_MODEL, DM_TILES),
                  _const_spec(g.shape), col_tile(D_MODEL, 2 * FF_TILES), _const_spec(cw.shape), _const_spec(cb.shape),
                  row_tile(D_MODEL, FF_TILES)],
        out_specs=ffn_row(D_MODEL),
        out_shape=jax.ShapeDtypeStruct((T, D_MODEL), F32),
        scratch_shapes=[pltpu.VMEM((DM_TILES, QA_W, MXU_N), BF16),
                        pltpu.VMEM((DM_TILES, VB_W, MXU_N), BF16),
                        pltpu.VMEM((DM_TILES, D_MODEL, MXU_N), BF16),
                        pltpu.VMEM((2 * FF_TILES, D_MODEL, MXU_N), BF16),
                        pltpu.VMEM((FF_TILES, MXU_N, D_MODEL), BF16),
                        pltpu.VMEM((H_B, DK_B, DV_B), F32),
                        pltpu.VMEM((2, EX_ROWS, QKB_W), F32),
                        pltpu.VMEM((2, TM_FFN, VB_W), BF16),
                        pltpu.VMEM((TM_FFN + HALO, D_FF), F32),
                        pltpu.VMEM((TM_FFN, D_MODEL), BF16),
                        pltpu.VMEM((TM_FFN, D_MODEL), F32),
                        pltpu.VMEM((TM_FFN, D_MODEL), BF16),
                        pltpu.VMEM((TM_FFN, D_FF), BF16)],
        compiler_params=pltpu.CompilerParams(dimension_semantics=("arbitrary",), vmem_limit_bytes=VMEM_LIMIT),
        name="gla_mix_ffn",
    )(q, k, v, gk, rb, tab, masks, gn, x, ya, gates, wa, wb, wo, g, wi, cw, cb, wf)


def kernel(x, ln_mix_g, w_in, q_norm_g, k_norm_g, sinks, w_gk_up, b_gk, gla_norm_g, w_branch_a, w_branch_b,
           w_out, ln_ffn_g, w_ffn_in, conv_w, conv_b, w_ffn_out):
    B, S, D = x.shape
    assert D == D_MODEL and S % TM_IN == 0 and S % TM_FFN == 0
    T = B * S
    xt = x.reshape(T, D)
    head = np.arange(QA_W)[:, None] // DH_A == np.arange(QA_W)[None, :] // DH_A
    bd = jnp.asarray(head.astype(np.float32), BF16)
    tab, masks = _gla_tables()
    wup = jnp.concatenate([w_gk_up, jnp.zeros((DEPTH, LANES - GK_RANK, QKB_W), F32)], axis=1).astype(BF16)
    for l in range(DEPTH):
        qg = (jnp.tile(q_norm_g[l], HQ_A) * (DH_A ** -0.5)).reshape(1, QA_W)
        kg = jnp.tile(k_norm_g[l], HKV_A).reshape(1, KVA_W)
        ya, qb, kb, vb, rb, gates, gk = _in_proj_swa(
            l, sinks[l], xt, ln_mix_g[l].reshape(1, D), w_in, bd, qg, kg, wup[l], b_gk[l].reshape(1, QKB_W), B)
        xt = _gla_mix_ffn(l, qb, kb, vb, gk, rb, tab, masks, gla_norm_g[l].reshape(1, DV_B),
                          xt, ya, gates, w_branch_a, w_branch_b, w_out, ln_ffn_g[l].reshape(1, D), w_ffn_in,
                          conv_w[l], conv_b[l].reshape(1, D_FF), w_ffn_out, B)
    return xt.reshape(B, S, D)
```

```python
import functools

import numpy as np
import jax
import jax.numpy as jnp
from jax import lax
from jax.experimental import pallas as pl
from jax.experimental.pallas import tpu as pltpu

F32 = jnp.float32
BF16 = jnp.bfloat16

D_MODEL = 1024
DEPTH = 2
HQ_A, HKV_A, DH_A = 8, 2, 64
G_A = HQ_A // HKV_A
WINDOW = 128
BLOCK_A = 128
H_B, DK_B, DV_B = 4, 128, 256
GK_RANK = 16
GATE_NORMALIZER = 16.0
CHUNK_B = 64
D_FF = 2816
CONV_W = 3
EPS = 1e-6

QA_W = HQ_A * DH_A
KVA_W = HKV_A * DH_A
QKB_W = H_B * DK_B
VB_W = H_B * DV_B
LANES = 128
MXU_N = 256
IN_W = QA_W + 2 * KVA_W + 2 * QKB_W + 2 * VB_W + GK_RANK + 2 * D_MODEL
_T_QA = 0
_T_KV = _T_QA + QA_W // MXU_N
_T_QB = _T_KV + 2 * KVA_W // MXU_N
_T_KB = _T_QB + QKB_W // MXU_N
_T_VB = _T_KB + QKB_W // MXU_N
_T_RB = _T_VB + VB_W // MXU_N
_T_GATES = _T_RB + VB_W // MXU_N
N_W_TILES = -(-IN_W // MXU_N)
N_GATE_FULL = N_W_TILES - 1 - _T_GATES
GATES_W = N_GATE_FULL * MXU_N + LANES
_G_A = GK_RANK
_G_B = GK_RANK + D_MODEL
assert (_T_GATES * MXU_N, IN_W - (N_W_TILES - 1) * MXU_N) == (QA_W + 2 * KVA_W + 2 * QKB_W + 2 * VB_W, GK_RANK)

VMEM_LIMIT = 58 * 1024 * 1024

TM_IN = 512
TM_GLA = 256
TM_MIX = 512
N_LEVELS = 6
N_BLK = TM_IN // BLOCK_A
N_CHUNK = TM_GLA // CHUNK_B
EX_ROWS = (2 + N_LEVELS) * CHUNK_B
HALO = 8
NT_DIMS = (((1,), (1,)), ((), ()))
FF_TILES = D_FF // MXU_N
DM_TILES = D_MODEL // MXU_N
NW_IN = N_W_TILES
NW_FFN = FF_TILES * 2
ATT_SKEW = 2
GLA_SKEW = 1
PROLOGUE_FILL = (3, 2)


def _const_spec(shape):
    nd = len(shape)
    return pl.BlockSpec(shape, lambda *_: (0,) * nd, pipeline_mode=pl.Buffered(1))


def _weave(main, filler):
    owed = 0.0
    for amount in main:
        owed += amount
        while owed >= 1.0:
            owed -= 1.0
            next(filler, None)
    for _ in filler:
        pass


def _in_proj_swa_kernel(tiles_per_seq, sinks_ref, x_ref, g_ref, wblk_ref, bd_ref, qg_ref, kg_ref, wup_ref, bgk_ref,
                        ya_ref, qb_ref, kb_ref, vb_ref, rb_ref, gates_ref, gk_ref, w_s, h_s, q_s, k_s, v_s):
    step = pl.program_id(0)

    @pl.when(step < NW_IN)
    def _():
        row = lax.broadcasted_iota(jnp.int32, (MXU_N, D_MODEL), 0) + step * MXU_N
        w_s[step] = jnp.where(row < IN_W, wblk_ref[...], 0.0).T.astype(BF16)

    @pl.when(step >= NW_IN)
    def _():
        _in_proj_swa_compute(step - NW_IN, tiles_per_seq, sinks_ref, x_ref, g_ref, bd_ref, qg_ref, kg_ref, wup_ref,
                             bgk_ref, ya_ref, qb_ref, kb_ref, vb_ref, rb_ref, gates_ref, gk_ref,
                             w_s, h_s, q_s, k_s, v_s)


def _in_proj_swa_compute(i, tiles_per_seq, sinks_ref, x_ref, g_ref, bd_ref, qg_ref, kg_ref, wup_ref, bgk_ref,
                         ya_ref, qb_ref, kb_ref, vb_ref, rb_ref, gates_ref, gk_ref, w_s, h_s, q_s, k_s, v_s):
    first = (i % tiles_per_seq) == 0

    @pl.when(i == 0)
    def _():
        v_s[...] = jnp.ones(v_s.shape, BF16)

    @pl.when(first)
    def _():
        k_s[0:BLOCK_A, :] = jnp.zeros((BLOCK_A, KVA_W), BF16)
        v_s[0:BLOCK_A, :] = jnp.zeros((BLOCK_A, 2 * KVA_W), BF16)

    x = x_ref[...]
    ms = jnp.mean(x * x, axis=-1, keepdims=True)
    h_s[...] = (x * lax.rsqrt(ms + EPS) * g_ref[...]).astype(BF16)

    def proj(t, width=MXU_N):
        return jnp.dot(h_s[...], w_s[t][:, 0:width], preferred_element_type=F32)

    def head_norm(z, gain, width):
        ss = jnp.dot((z * z).astype(BF16), bd_ref[:width, :width], preferred_element_type=F32)
        return z * lax.rsqrt(ss * (1.0 / DH_A) + EPS) * gain

    n_proj = N_W_TILES - _T_QB

    def attention():
        qpos = lax.broadcasted_iota(jnp.int32, (BLOCK_A, 2 * BLOCK_A), 0)
        kpos = lax.broadcasted_iota(jnp.int32, (BLOCK_A, 2 * BLOCK_A), 1) - BLOCK_A
        band = (kpos <= qpos) & (kpos > qpos - WINDOW)
        first_valid = band & (kpos >= jnp.where(first, 0, -BLOCK_A))

        cur = slice(BLOCK_A, BLOCK_A + TM_IN)
        zq = jnp.concatenate([proj(_T_QA + n) for n in range(QA_W // MXU_N)], axis=1)
        kv = proj(_T_KV)
        yield PROLOGUE_FILL[0]
        q_s[...] = head_norm(zq, qg_ref[...], QA_W).astype(BF16)
        k_s[cur, :] = head_norm(kv[:, 0:KVA_W], kg_ref[...], KVA_W).astype(BF16)
        va = kv[:, KVA_W:2 * KVA_W].astype(BF16)
        for hk in range(HKV_A):
            v_s[cur, 2 * hk * DH_A:(2 * hk + 1) * DH_A] = va[:, hk * DH_A:(hk + 1) * DH_A]
        yield PROLOGUE_FILL[1]

        def scores(j, hk, g):
            rq = slice(j * BLOCK_A, (j + 1) * BLOCK_A)
            rk = slice(j * BLOCK_A, (j + 2) * BLOCK_A)
            qs = slice((hk * G_A + g) * DH_A, (hk * G_A + g + 1) * DH_A)
            return lax.dot_general(q_s[rq, qs], k_s[rk, hk * DH_A:(hk + 1) * DH_A], NT_DIMS,
                                   preferred_element_type=F32)

        def finish(j, hk, g, s):
            rq = slice(j * BLOCK_A, (j + 1) * BLOCK_A)
            rk = slice(j * BLOCK_A, (j + 2) * BLOCK_A)
            hq = hk * G_A + g
            qs = slice(hq * DH_A, (hq + 1) * DH_A)
            s = jnp.where(first_valid if j == 0 else band, s, -jnp.inf)
            sink = sinks_ref[hq]
            m = jnp.maximum(jnp.max(s, axis=-1, keepdims=True), sink)
            p = jnp.exp(s - m).astype(BF16)
            vv = v_s[rk, 2 * hk * DH_A:(2 * hk + 2) * DH_A]
            oe = jnp.dot(p, vv, preferred_element_type=F32)
            den = oe[:, DH_A:DH_A + 1] + jnp.exp(sink - m)
            ya_ref[rq, qs] = (oe[:, 0:DH_A] / den).astype(BF16)

        per_group = (n_proj - sum(PROLOGUE_FILL)) / (2 * N_BLK * HQ_A)
        pending = []
        for j in range(N_BLK):
            for hk in range(HKV_A):
                for g in range(G_A):
                    pending.append((j, hk, g, scores(j, hk, g)))
                    yield per_group
                    if len(pending) > ATT_SKEW:
                        finish(*pending.pop(0))
                        yield per_group
            if j == 1:
                k_s[0:BLOCK_A, :] = k_s[TM_IN:TM_IN + BLOCK_A, :]
                v_s[0:BLOCK_A, :] = v_s[TM_IN:TM_IN + BLOCK_A, :]
        while pending:
            finish(*pending.pop(0))
            yield per_group

    def projections():
        z = proj(_T_GATES)
        gates_ref[:, 0:MXU_N] = jax.nn.sigmoid(z).astype(BF16)
        lr = z[:, 0:LANES].astype(BF16)
        t = jnp.dot(lr, wup_ref[...], preferred_element_type=F32) + bgk_ref[...]
        log_sig = jnp.minimum(t, 0.0) - jnp.log1p(jnp.exp(-jnp.abs(t)))
        gk_ref[...] = log_sig * (1.0 / GATE_NORMALIZER)
        yield
        for n in range(1, N_GATE_FULL):
            gates_ref[:, n * MXU_N:(n + 1) * MXU_N] = jax.nn.sigmoid(proj(_T_GATES + n)).astype(BF16)
            yield
        gates_ref[:, N_GATE_FULL * MXU_N:GATES_W] = jax.nn.sigmoid(proj(N_W_TILES - 1, LANES)).astype(BF16)
        yield
        plan = ((_T_QB, QKB_W, qb_ref, lambda z: z * (DK_B ** -0.5)), (_T_KB, QKB_W, kb_ref, None),
                (_T_VB, VB_W, vb_ref, None), (_T_RB, VB_W, rb_ref, None))
        for t0, width, ref, fn in plan:
            for n in range(width // MXU_N):
                z = proj(t0 + n)
                ref[:, n * MXU_N:(n + 1) * MXU_N] = (z if fn is None else fn(z)).astype(BF16)
                yield

    _weave(attention(), projections())


def _in_proj_swa(layer, sinks, x, g, w_in, bd, qg, kg, wup, bgk, batch):
    T = x.shape[0]
    row = lambda width: pl.BlockSpec((TM_IN, width), lambda s: (jnp.maximum(s - NW_IN, 0), 0))
    widths = (QA_W, QKB_W, QKB_W, VB_W, VB_W, GATES_W)
    out_shape = [jax.ShapeDtypeStruct((T, wd), BF16) for wd in widths] + [jax.ShapeDtypeStruct((T, QKB_W), F32)]
    out_specs = [row(wd) for wd in widths] + [row(QKB_W)]
    w_tile = pl.BlockSpec((None, MXU_N, D_MODEL), lambda s: (layer, jnp.minimum(s, NW_IN - 1), 0))
    return pl.pallas_call(
        functools.partial(_in_proj_swa_kernel, T // batch // TM_IN),
        grid=(NW_IN + T // TM_IN,),
        in_specs=[pl.BlockSpec(memory_space=pltpu.SMEM), row(D_MODEL), _const_spec(g.shape), w_tile,
                  _const_spec(bd.shape), _const_spec(qg.shape), _const_spec(kg.shape), _const_spec(wup.shape),
                  _const_spec(bgk.shape)],
        out_specs=out_specs,
        out_shape=out_shape,
        scratch_shapes=[pltpu.VMEM((N_W_TILES, D_MODEL, MXU_N), BF16),
                        pltpu.VMEM((TM_IN, D_MODEL), BF16),
                        pltpu.VMEM((TM_IN, QA_W), BF16),
                        pltpu.VMEM((BLOCK_A + TM_IN, KVA_W), BF16),
                        pltpu.VMEM((BLOCK_A + TM_IN, 2 * KVA_W), BF16)],
        compiler_params=pltpu.CompilerParams(dimension_semantics=("arbitrary",), vmem_limit_bytes=VMEM_LIMIT),
        name="in_proj_swa",
    )(sinks, x, g, jnp.swapaxes(w_in, 1, 2), bd, qg, kg, wup, bgk)


def _gla_tables():
    C = CHUNK_B
    t = np.arange(C)[:, None]
    u = np.arange(C)[None, :]
    slabs = [(u <= t), (u > t)]
    masks = []
    for l in range(N_LEVELS):
        m = (C // 2) >> l
        base = t - (t % (2 * m))
        right = (t % (2 * m)) >= m
        sel_r = (u >= base + m) & (u <= t)
        sel_l = (u > t) & (u <= base + m - 1)
        slabs.append(np.where(right, sel_r, sel_l))
        same_pair = (t // (2 * m)) == (u // (2 * m))
        masks.append(same_pair & right & ((u % (2 * m)) < m))
    masks.append(t == u)
    table = np.concatenate(slabs, axis=0).astype(np.float32)
    table3 = np.concatenate([table, table, table], axis=1)
    return jnp.asarray(table3, BF16), jnp.asarray(np.stack(masks).astype(np.float32))


def _gla_kernel(tiles_per_seq, q_ref, k_ref, v_ref, gk_ref, rb_ref, tab_ref, mask_ref, gn_ref, yb_ref,
                state_ref, ex_ref):
    i = pl.program_id(0)

    @pl.when(i % tiles_per_seq == 0)
    def _():
        state_ref[...] = jnp.zeros_like(state_ref)

    def exponents(c):
        rows = slice(c * CHUNK_B, (c + 1) * CHUNK_B)
        gk = gk_ref[rows, :]
        hi = gk.astype(BF16)
        r1 = gk - hi.astype(F32)
        mid = r1.astype(BF16)
        lo = (r1 - mid.astype(F32)).astype(BF16)
        g3 = jnp.concatenate([hi, mid, lo], axis=0)
        ex_ref[c % 2] = jnp.exp(jnp.dot(tab_ref[...], g3, preferred_element_type=F32))

    def prep(c, h):
        rows = slice(c * CHUNK_B, (c + 1) * CHUNK_B)
        kc = slice(h * DK_B, (h + 1) * DK_B)
        ex = ex_ref.at[c % 2]
        qb16 = q_ref[rows, kc]
        kb16 = k_ref[rows, kc]
        qh = qb16.astype(F32)
        kh = kb16.astype(F32)
        pairs = []
        for l in range(N_LEVELS):
            e = ex[(2 + l) * CHUNK_B:(3 + l) * CHUNK_B, kc]
            pairs.append(((qh * e).astype(BF16), (kh * e).astype(BF16)))
        pairs.append((qb16, kb16))
        qe = (qh * ex[0:CHUNK_B, kc]).astype(BF16)
        kdt = (kh * ex[CHUNK_B:2 * CHUNK_B, kc]).T.astype(BF16)
        decay = ex[CHUNK_B - 8:CHUNK_B, kc].T[:, 7:8]
        return pairs, qe, kdt, decay

    def levels(pairs):
        return [lax.dot_general(ql, kl, NT_DIMS, preferred_element_type=F32) for ql, kl in pairs]

    def outputs(c, h, raw, qe, kdt, decay):
        rows = slice(c * CHUNK_B, (c + 1) * CHUNK_B)
        vc = slice(h * DV_B, (h + 1) * DV_B)
        a = mask_ref[0] * raw[0]
        for l in range(1, N_LEVELS + 1):
            a = a + mask_ref[l] * raw[l]
        vh = v_ref[rows, vc]
        s_prev = state_ref[h]
        o = (jnp.dot(a.astype(BF16), vh, preferred_element_type=F32)
             + jnp.dot(qe, s_prev.astype(BF16), preferred_element_type=F32))
        state_ref[h] = decay * s_prev + jnp.dot(kdt, vh, preferred_element_type=F32)
        ms = jnp.mean(o * o, axis=-1, keepdims=True)
        r = rb_ref[rows, vc].astype(F32)
        y = o * lax.rsqrt(ms + EPS) * gn_ref[...] * (r * jax.nn.sigmoid(r))
        yb_ref[rows, vc] = y.astype(BF16)

    heads = [(c, h) for c in range(N_CHUNK) for h in range(H_B)]
    exponents(0)
    prepped = [prep(*heads[0])]
    in_flight = []
    for idx, (c, h) in enumerate(heads):
        if h == 0 and c + 1 < N_CHUNK:
            exponents(c + 1)
        pairs, qe, kdt, decay = prepped.pop(0)
        if idx + 1 < len(heads):
            prepped.append(prep(*heads[idx + 1]))
        in_flight.append((c, h, levels(pairs), qe, kdt, decay))
        if len(in_flight) > GLA_SKEW:
            outputs(*in_flight.pop(0))
    while in_flight:
        outputs(*in_flight.pop(0))


def _gla(q, k, v, gk, rb, tab, masks, gn, batch):
    T = q.shape[0]
    row = lambda width: pl.BlockSpec((TM_GLA, width), lambda i: (i, 0))
    return pl.pallas_call(
        functools.partial(_gla_kernel, T // batch // TM_GLA),
        grid=(T // TM_GLA,),
        in_specs=[row(QKB_W), row(QKB_W), row(VB_W), row(QKB_W), row(VB_W),
                  _const_spec(tab.shape), _const_spec(masks.shape), _const_spec(gn.shape)],
        out_specs=row(VB_W),
        out_shape=jax.ShapeDtypeStruct((T, VB_W), BF16),
        scratch_shapes=[pltpu.VMEM((H_B, DK_B, DV_B), F32),
                        pltpu.VMEM((2, EX_ROWS, QKB_W), F32)],
        compiler_params=pltpu.CompilerParams(dimension_semantics=("arbitrary",), vmem_limit_bytes=VMEM_LIMIT),
        name="gla",
    )(q, k, v, gk, rb, tab, masks, gn)


FF_GROUPS = ((0, 5 * MXU_N), (5 * MXU_N, 6 * MXU_N))
assert sum(w for _, w in FF_GROUPS) == D_FF
FF_GROUP_MAX = max(w for _, w in FF_GROUPS)


def _mix_ffn_kernel(tiles_per_seq, x_ref, ya_ref, yb_ref, gates_ref, wa_blk, wb_blk, wo_blk, g_ref, wi_blk, cw_ref,
                    cb_ref, wf_blk, o_ref, wa_s, wb_s, wo_s, wi_s, wf_s, halo_s, ct_s, mg_s, x1_s, hn_s, act_s):
    step = pl.program_id(0)

    @pl.when(step < NW_FFN)
    def _():
        for n in range(2 * FF_TILES):
            @pl.when(step == n)
            def _():
                wi_s[:, n * MXU_N:(n + 1) * MXU_N] = wi_blk[...].astype(BF16)

        @pl.when(step < FF_TILES)
        def _():
            wf_s[pl.ds(pl.multiple_of(step * MXU_N, MXU_N), MXU_N), :] = wf_blk[...].astype(BF16)

        @pl.when(step < DM_TILES)
        def _():
            wa_s[step] = wa_blk[...].astype(BF16)
            wb_s[step] = wb_blk[...].astype(BF16)
            wo_s[step] = wo_blk[...].astype(BF16)

    @pl.when(step >= NW_FFN)
    def _():
        i = step - NW_FFN

        @pl.when(i % tiles_per_seq == 0)
        def _():
            halo_s[...] = jnp.zeros((HALO, D_FF), F32)

        tile = lambda n: slice(n * MXU_N, (n + 1) * MXU_N)
        gate = lambda off, n: gates_ref[:, off + n * MXU_N:off + (n + 1) * MXU_N].astype(F32)
        for n in range(DM_TILES):
            pa = jnp.dot(ya_ref[...], wa_s[n], preferred_element_type=F32)
            pb = jnp.dot(yb_ref[...], wb_s[n], preferred_element_type=F32)
            mg_s[:, tile(n)] = (gate(_G_A, n) * pa + gate(_G_B, n) * pb).astype(BF16)
        for n in range(DM_TILES):
            x1_s[:, tile(n)] = x_ref[:, tile(n)] + jnp.dot(mg_s[...], wo_s[n], preferred_element_type=F32)
        x1 = x1_s[...]
        ms = jnp.mean(x1 * x1, axis=-1, keepdims=True)
        hn_s[...] = (x1 * lax.rsqrt(ms + EPS) * g_ref[...]).astype(BF16)
        for c0, w in FF_GROUPS:
            cols = slice(c0, c0 + w)
            ct_s[0:HALO, 0:w] = halo_s[:, cols]
            ct_s[HALO:HALO + TM_MIX, 0:w] = jnp.dot(hn_s[...], wi_s[:, cols], preferred_element_type=F32)
            u = jnp.dot(hn_s[...], wi_s[:, D_FF + c0:D_FF + c0 + w], preferred_element_type=F32)
            conv = cb_ref[:, cols]
            for j in range(CONV_W):
                lo_r = HALO - (CONV_W - 1) + j
                conv = conv + cw_ref[j:j + 1, cols] * ct_s[lo_r:lo_r + TM_MIX, 0:w]
            act_s[:, cols] = (conv * jax.nn.sigmoid(conv) * u).astype(BF16)
            halo_s[:, cols] = ct_s[TM_MIX:TM_MIX + HALO, 0:w]
        o_ref[...] = x1_s[...] + jnp.dot(act_s[...], wf_s[...], preferred_element_type=F32)


def _mix_ffn(layer, x, ya, yb, gates, wa, wb, wo, g, wi, cw, cb, wf, batch):
    T = x.shape[0]
    n_tiles = T // TM_MIX
    row = lambda width: pl.BlockSpec((TM_MIX, width), lambda s: (jnp.maximum(s - NW_FFN, 0), 0))
    col_tile = lambda rows, n: pl.BlockSpec((None, rows, MXU_N), lambda s: (layer, 0, jnp.minimum(s, n - 1)))
    row_tile = lambda cols, n: pl.BlockSpec((None, MXU_N, cols), lambda s: (layer, jnp.minimum(s, n - 1), 0))
    return pl.pallas_call(
        functools.partial(_mix_ffn_kernel, T // batch // TM_MIX),
        grid=(NW_FFN + n_tiles,),
        in_specs=[row(D_MODEL), row(QA_W), row(VB_W), row(GATES_W),
                  col_tile(QA_W, DM_TILES), col_tile(VB_W, DM_TILES), col_tile(D_MODEL, DM_TILES),
                  _const_spec(g.shape), col_tile(D_MODEL, 2 * FF_TILES), _const_spec(cw.shape), _const_spec(cb.shape),
                  row_tile(D_MODEL, FF_TILES)],
        out_specs=row(D_MODEL),
        out_shape=jax.ShapeDtypeStruct((T, D_MODEL), F32),
        scratch_shapes=[pltpu.VMEM((DM_TILES, QA_W, MXU_N), BF16),
                        pltpu.VMEM((DM_TILES, VB_W, MXU_N), BF16),
                        pltpu.VMEM((DM_TILES, D_MODEL, MXU_N), BF16),
                        pltpu.VMEM((D_MODEL, 2 * D_FF), BF16),
                        pltpu.VMEM((D_FF, D_MODEL), BF16),
                        pltpu.VMEM((HALO, D_FF), F32),
                        pltpu.VMEM((TM_MIX + HALO, FF_GROUP_MAX), F32),
                        pltpu.VMEM((TM_MIX, D_MODEL), BF16),
                        pltpu.VMEM((TM_MIX, D_MODEL), F32),
                        pltpu.VMEM((TM_MIX, D_MODEL), BF16),
                        pltpu.VMEM((TM_MIX, D_FF), BF16)],
        compiler_params=pltpu.CompilerParams(dimension_semantics=("arbitrary",), vmem_limit_bytes=VMEM_LIMIT),
        name="mix_ffn",
    )(x, ya, yb, gates, wa, wb, wo, g, wi, cw, cb, wf)


def kernel(x, ln_mix_g, w_in, q_norm_g, k_norm_g, sinks, w_gk_up, b_gk, gla_norm_g, w_branch_a, w_branch_b,
           w_out, ln_ffn_g, w_ffn_in, conv_w, conv_b, w_ffn_out):
    B, S, D = x.shape
    assert D == D_MODEL and S % TM_IN == 0 and S % TM_GLA == 0 and S % TM_MIX == 0
    T = B * S
    xt = x.reshape(T, D)
    head = np.arange(QA_W)[:, None] // DH_A == np.arange(QA_W)[None, :] // DH_A
    bd = jnp.asarray(head.astype(np.float32), BF16)
    tab, masks = _gla_tables()
    wup = jnp.concatenate([w_gk_up, jnp.zeros((DEPTH, LANES - GK_RANK, QKB_W), F32)], axis=1).astype(BF16)
    for l in range(DEPTH):
        qg = (jnp.tile(q_norm_g[l], HQ_A) * (DH_A ** -0.5)).reshape(1, QA_W)
        kg = jnp.tile(k_norm_g[l], HKV_A).reshape(1, KVA_W)
        ya, qb, kb, vb, rb, gates, gk = _in_proj_swa(
            l, sinks[l], xt, ln_mix_g[l].reshape(1, D), w_in, bd, qg, kg, wup[l], b_gk[l].reshape(1, QKB_W), B)
        yb = _gla(qb, kb, vb, gk, rb, tab, masks, gla_norm_g[l].reshape(1, DV_B), B)
        xt = _mix_ffn(l, xt, ya, yb, gates, w_branch_a, w_branch_b, w_out, ln_ffn_g[l].reshape(1, D), w_ffn_in,
                      conv_w[l], conv_b[l].reshape(1, D_FF), w_ffn_out, B)
    return xt.reshape(B, S, D)
```
